```python
import jax, jax.numpy as jnp
from jax import lax
import numpy as np

D_MODEL = 1024
BATCH = 4
SEQ = 8192
DEPTH = 1

HEAD_DIM = 64
ATTN_HEADS_PER_GROUP = 8
DILATED_GROUPS = ((128, 1), (512, 4), (2048, 16))
N_DIL = len(DILATED_GROUPS)
ATTN_WIDTH = ATTN_HEADS_PER_GROUP * HEAD_DIM
ROPE_DIM = HEAD_DIM // 4
ROPE_THETA = 500000.0
BLK = 128
SGU_CHUNK = 128
SGU_GROUPS = 8
SGU_WIDTH = D_MODEL // 2
SGU_GROUP_DIM = SGU_WIDTH // SGU_GROUPS
D_FF = -(-8 * D_MODEL // (3 * 256)) * 256
QKV_COLS = 3 * N_DIL * ATTN_WIDTH
IN_COLS = QKV_COLS + 2 * SGU_WIDTH + 2 * D_MODEL
EPS = 1e-6

kernel_name = "hybrid_dilated_attn_gmlp_gated_block"


def rmsnorm(x, g):
    xf = x.astype(jnp.float32)
    y = xf * lax.rsqrt(jnp.mean(xf * xf, axis=-1, keepdims=True) + EPS)
    return (y * g.astype(jnp.float32)).astype(x.dtype)


def layernorm(x, g, b):
    xf = x.astype(jnp.float32)
    mu = jnp.mean(xf, axis=-1, keepdims=True)
    xc = xf - mu
    y = xc * lax.rsqrt(jnp.mean(xc * xc, axis=-1, keepdims=True) + EPS)
    return (y * g.astype(jnp.float32) + b.astype(jnp.float32)).astype(x.dtype)


def partial_rope(t, positions):
    half = ROPE_DIM // 2
    inv_freq = ROPE_THETA ** (-jnp.arange(0, ROPE_DIM, 2, dtype=jnp.float32) / ROPE_DIM)
    ang = positions.astype(jnp.float32)[..., None] * inv_freq
    cos = jnp.cos(ang)[:, :, None, :]
    sin = jnp.sin(ang)[:, :, None, :]
    tf = t.astype(jnp.float32)
    x1, x2 = tf[..., :half], tf[..., half:ROPE_DIM]
    rot = jnp.concatenate([x1 * cos - x2 * sin, x2 * cos + x1 * sin, tf[..., ROPE_DIM:]], axis=-1)
    return rot.astype(t.dtype)


def dilated_attention(q, k, v, window, dilation):
    B, S, H, Dh = q.shape
    span = window // dilation
    L = -(-S // dilation)
    L_pad = -(-L // BLK) * BLK
    S_pad = L_pad * dilation
    nb = L_pad // BLK
    pad = ((0, 0), (0, S_pad - S), (0, 0), (0, 0))

    def strided(t):
        t = jnp.pad(t, pad).reshape(B, L_pad, dilation, H, Dh).transpose(0, 2, 1, 3, 4)
        return t.reshape(B, dilation, nb, BLK, H, Dh)

    def with_prev(t):
        prev = jnp.pad(t, ((0, 0), (0, 0), (1, 0), (0, 0), (0, 0), (0, 0)))[:, :, :-1]
        return jnp.concatenate([prev, t], axis=3)

    qs = strided(q * (Dh ** -0.5))
    kb = with_prev(strided(k))
    vb = with_prev(strided(v))
    s = jnp.einsum('brnqhd,brnkhd->brnhqk', qs, kb, preferred_element_type=jnp.float32)

    i = jnp.arange(BLK)[:, None]
    j = jnp.arange(2 * BLK)[None, :]
    diff = BLK + i - j
    band = (diff >= 0) & (diff <= span)
    key_exists = (jnp.arange(nb)[:, None, None] > 0) | (j >= BLK)[None]
    mask = band[None] & key_exists
    s = jnp.where(mask[None, None, :, None], s, -jnp.inf)

    m = jnp.max(s, axis=-1, keepdims=True)
    p = jnp.exp(s - m)
    den = jnp.sum(p, axis=-1)
    lse = m[..., 0] + jnp.log(den)
    o = jnp.einsum('brnhqk,brnkhd->brnqhd', p, vb.astype(jnp.float32))
    o = o / jnp.swapaxes(den, -1, -2)[..., None]

    o = o.reshape(B, dilation, L_pad, H, Dh).transpose(0, 2, 1, 3, 4).reshape(B, S_pad, H, Dh)[:, :S]
    lse = jnp.swapaxes(lse, -1, -2).reshape(B, dilation, L_pad, H).transpose(0, 2, 1, 3)
    lse = lse.reshape(B, S_pad, H)[:, :S]
    return o.astype(q.dtype), lse


def spatial_gating(uv, ln_g, ln_b, w_s, b_s):
    B, S, _ = uv.shape
    z = jax.nn.gelu(uv, approximate=False)
    u, v = z[..., :SGU_WIDTH], z[..., SGU_WIDTH:]
    v = layernorm(v, ln_g, ln_b)
    vc = v.reshape(B, S // SGU_CHUNK, SGU_CHUNK, SGU_GROUPS, SGU_GROUP_DIM)
    causal = jnp.tril(jnp.ones((SGU_CHUNK, SGU_CHUNK), dtype=bool))
    w_causal = jnp.where(causal[None], w_s, jnp.zeros_like(w_s))
    mixed = jnp.einsum('gts,bnsgc->bntgc', w_causal, vc)
    mixed = mixed + jnp.transpose(b_s)[None, None, :, :, None]
    return u * mixed.reshape(B, S, SGU_WIDTH)


def setup_inputs(seed: int = 0) -> dict:
    key = jax.random.key(seed)
    ks = jax.random.split(key, 18)
    f32 = jnp.float32
    x = jax.random.normal(ks[0], (BATCH, SEQ, D_MODEL), f32)
    offset = jax.random.randint(ks[1], (BATCH, 1), 0, 4096, dtype=jnp.int32)
    positions = offset + jnp.arange(SEQ, dtype=jnp.int32)[None, :]
    nrm = lambda k, shape, fan_in: jax.random.normal(k, shape, f32) * (fan_in ** -0.5)
    return {
        "x": x,
        "positions": positions,
        "norm1_g": 1.0 + 0.02 * jax.random.normal(ks[2], (DEPTH, D_MODEL), f32),
        "w_in": nrm(ks[3], (DEPTH, D_MODEL, IN_COLS), D_MODEL),
        "sgu_ln_g": 1.0 + 0.02 * jax.random.normal(ks[4], (DEPTH, SGU_WIDTH), f32),
        "sgu_ln_b": 0.02 * jax.random.normal(ks[5], (DEPTH, SGU_WIDTH), f32),
        "w_spatial": nrm(ks[6], (DEPTH, SGU_GROUPS, SGU_CHUNK, SGU_CHUNK), SGU_CHUNK),
        "b_spatial": 1.0 + 0.1 * jax.random.normal(ks[7], (DEPTH, SGU_GROUPS, SGU_CHUNK), f32),
        "w_proj_attn": nrm(ks[8], (DEPTH, ATTN_WIDTH, D_MODEL), ATTN_WIDTH),
        "w_proj_sgu": nrm(ks[9], (DEPTH, SGU_WIDTH, D_MODEL), SGU_WIDTH),
        "w_out": nrm(ks[10], (DEPTH, D_MODEL, D_MODEL), D_MODEL),
        "norm2_g": 1.0 + 0.02 * jax.random.normal(ks[11], (DEPTH, D_MODEL), f32),
        "w_ffn_gate": nrm(ks[12], (DEPTH, D_MODEL, D_FF), D_MODEL),
        "w_ffn_up": nrm(ks[13], (DEPTH, D_MODEL, D_FF), D_MODEL),
        "w_ffn_down": nrm(ks[14], (DEPTH, D_FF, D_MODEL), D_FF),
        "final_g": 1.0 + 0.02 * jax.random.normal(ks[15], (D_MODEL,), f32),
    }


def reference(x, positions, norm1_g, w_in, sgu_ln_g, sgu_ln_b, w_spatial, b_spatial,
              w_proj_attn, w_proj_sgu, w_out, norm2_g, w_ffn_gate, w_ffn_up, w_ffn_down,
              final_g):
    B, S, _ = x.shape
    for l in range(DEPTH):
        h = rmsnorm(x, norm1_g[l])
        proj = h @ w_in[l]
        qkv = proj[..., :QKV_COLS].reshape(B, S, 3, N_DIL, ATTN_HEADS_PER_GROUP, HEAD_DIM)
        uv = proj[..., QKV_COLS:QKV_COLS + 2 * SGU_WIDTH]
        gate_a = jax.nn.sigmoid(proj[..., QKV_COLS + 2 * SGU_WIDTH:QKV_COLS + 2 * SGU_WIDTH + D_MODEL])
        gate_b = jax.nn.sigmoid(proj[..., QKV_COLS + 2 * SGU_WIDTH + D_MODEL:])

        outs, lses = [], []
        for g, (window, dilation) in enumerate(DILATED_GROUPS):
            q = partial_rope(qkv[:, :, 0, g], positions)
            k = partial_rope(qkv[:, :, 1, g], positions)
            o, lse = dilated_attention(q, k, qkv[:, :, 2, g], window, dilation)
            outs.append(o)
            lses.append(lse)
        alpha = jax.nn.softmax(jnp.stack(lses, axis=0), axis=0)
        attn = jnp.sum(alpha[..., None].astype(x.dtype) * jnp.stack(outs, axis=0), axis=0)
        attn = attn.reshape(B, S, ATTN_WIDTH)

        sgu = spatial_gating(uv, sgu_ln_g[l], sgu_ln_b[l], w_spatial[l], b_spatial[l])

        merged = gate_a * (attn @ w_proj_attn[l]) + gate_b * (sgu @ w_proj_sgu[l])
        x = x + merged @ w_out[l]

        h2 = rmsnorm(x, norm2_g[l])
        ff = jax.nn.silu(h2 @ w_ffn_gate[l]) * (h2 @ w_ffn_up[l])
        x = x + ff @ w_ffn_down[l]
    return rmsnorm(x, final_g)
```

```python
import functools

import jax
import jax.numpy as jnp
import numpy as np
from jax import lax
from jax.experimental import pallas as pl
from jax.experimental.pallas import tpu as pltpu

D_MODEL = 1024
HEAD_DIM = 64
HEADS = 8
GROUP_DILATIONS = (1, 4, 16)
N_DIL = len(GROUP_DILATIONS)
SPAN = 128
ATTN_WIDTH = HEADS * HEAD_DIM
ROPE_DIM = HEAD_DIM // 4
ROPE_HALF = ROPE_DIM // 2
ROPE_THETA = 500000.0
SGU_CHUNK = 128
SGU_GROUPS = 8
SGU_WIDTH = D_MODEL // 2
SGU_GROUP_DIM = SGU_WIDTH // SGU_GROUPS
D_FF = 2816
QKV_COLS = 3 * N_DIL * ATTN_WIDTH
UV_OFF = QKV_COLS
GATE_OFF = QKV_COLS + 2 * SGU_WIDTH
IN_COLS = GATE_OFF + 2 * D_MODEL
EPS = 1e-6

LANES = 128
V7X_VMEM_BYTES = 64 * 1024 * 1024
VMEM_LIMIT = V7X_VMEM_BYTES - 8 * 1024 * 1024

IN_TM = 256
POST_TM = 256
ATTN_QB = 512
COL_CHUNK = 512

BF16 = jnp.bfloat16
F32 = jnp.float32


def _resident(shape):
    nd = len(shape)
    return pl.BlockSpec(shape, lambda *_: (0,) * nd, pipeline_mode=pl.Buffered(1))


def _dot(a, b):
    return jnp.dot(a, b, preferred_element_type=F32)


def _in_proj_kernel(x_ref, pos_ref, g1_ref, w_ref, freq_ref, lng_ref, lnb_ref, ws_ref, bs_ref,
                    qkv_ref, sgu_ref, gates_ref):
    tm = x_ref.shape[0]
    x = x_ref[...]
    h = (x * lax.rsqrt(jnp.mean(x * x, axis=-1, keepdims=True) + EPS) * g1_ref[...]).astype(BF16)

    ang = pos_ref[...].astype(F32) * freq_ref[...]
    cos = jnp.cos(ang)
    sin = jnp.sin(ang)
    lane = lax.broadcasted_iota(jnp.int32, (1, LANES), 1) % HEAD_DIM
    sin_up = jnp.where((lane >= ROPE_HALF) & (lane < ROPE_DIM), sin, 0.0)
    sin_dn = jnp.where(lane < ROPE_HALF, -sin, 0.0)

    qk_cols = 2 * N_DIL * ATTN_WIDTH
    for c in range(0, qk_cols, COL_CHUNK):
        y = _dot(h, w_ref[:, c:c + COL_CHUNK])
        scale = HEAD_DIM ** -0.5 if c < N_DIL * ATTN_WIDTH else 1.0
        for l in range(0, COL_CHUNK, LANES):
            t = y[:, l:l + LANES]
            r = (t * cos + pltpu.roll(t, ROPE_HALF, 1) * sin_up
                 + pltpu.roll(t, LANES - ROPE_HALF, 1) * sin_dn)
            qkv_ref[:, c + l:c + l + LANES] = (r * scale).astype(BF16)
    for c in range(qk_cols, QKV_COLS, COL_CHUNK):
        qkv_ref[:, c:c + COL_CHUNK] = _dot(h, w_ref[:, c:c + COL_CHUNK]).astype(BF16)

    def gelu(y):
        return 0.5 * y * (1.0 + lax.erf(y * np.float32(np.sqrt(0.5))))

    u = gelu(_dot(h, w_ref[:, UV_OFF:UV_OFF + SGU_WIDTH]))
    v = gelu(_dot(h, w_ref[:, UV_OFF + SGU_WIDTH:UV_OFF + 2 * SGU_WIDTH]))
    mu = jnp.mean(v, axis=-1, keepdims=True)
    vc = v - mu
    v = vc * lax.rsqrt(jnp.mean(vc * vc, axis=-1, keepdims=True) + EPS) * lng_ref[...] + lnb_ref[...]
    v = v.astype(BF16)

    row = lax.broadcasted_iota(jnp.int32, (SGU_CHUNK, SGU_GROUPS * SGU_CHUNK), 0)
    col = lax.broadcasted_iota(jnp.int32, (SGU_CHUNK, SGU_GROUPS * SGU_CHUNK), 1) % SGU_CHUNK
    w_causal = jnp.where(col <= row, ws_ref[...], jnp.zeros((), BF16))
    lane_group = lax.broadcasted_iota(jnp.int32, (1, SGU_WIDTH), 1) // SGU_GROUP_DIM
    for r0 in range(0, tm, SGU_CHUNK):
        vch = v[r0:r0 + SGU_CHUNK]
        v_bd = jnp.concatenate(
            [jnp.where(lane_group == g, vch, jnp.zeros((), BF16)) for g in range(SGU_GROUPS)], axis=0)
        mixed = _dot(w_causal, v_bd) + bs_ref[...]
        sgu_ref[r0:r0 + SGU_CHUNK, :] = (u[r0:r0 + SGU_CHUNK] * mixed).astype(BF16)

    for c in range(GATE_OFF, IN_COLS, COL_CHUNK):
        y = _dot(h, w_ref[:, c:c + COL_CHUNK])
        gates_ref[:, c - GATE_OFF:c - GATE_OFF + COL_CHUNK] = jax.nn.sigmoid(y).astype(BF16)


def _in_proj(x2, pos2, g1, w_in, freq, lng, lnb, ws_all, bs_full):
    n = x2.shape[0]
    row = lambda width: pl.BlockSpec((IN_TM, width), lambda i: (i, 0))
    return pl.pallas_call(
        _in_proj_kernel,
        grid=(n // IN_TM,),
        in_specs=[
            row(D_MODEL), row(1),
            _resident((1, D_MODEL)), _resident((D_MODEL, IN_COLS)), _resident((1, LANES)),
            _resident((1, SGU_WIDTH)), _resident((1, SGU_WIDTH)),
            _resident((SGU_CHUNK, SGU_GROUPS * SGU_CHUNK)), _resident((SGU_CHUNK, SGU_WIDTH)),
        ],
        out_specs=[row(QKV_COLS), row(SGU_WIDTH), row(2 * D_MODEL)],
        out_shape=[
            jax.ShapeDtypeStruct((n, QKV_COLS), BF16),
            jax.ShapeDtypeStruct((n, SGU_WIDTH), BF16),
            jax.ShapeDtypeStruct((n, 2 * D_MODEL), BF16),
        ],
        compiler_params=pltpu.CompilerParams(
            dimension_semantics=("arbitrary",), vmem_limit_bytes=VMEM_LIMIT),
        name="in_proj",
    )(x2, pos2, g1, w_in, freq, lng, lnb, ws_all, bs_full)


def _attn_kernel(q_ref, k_ref, kp_ref, v_ref, vp_ref, o_ref, lse_ref):
    first = pl.program_id(2) == 0
    qi = lax.broadcasted_iota(jnp.int32, (SPAN, 2 * SPAN), 0)
    kj = lax.broadcasted_iota(jnp.int32, (SPAN, 2 * SPAN), 1)
    band = (kj >= qi) & (kj <= qi + SPAN)
    first_lo = jnp.where(first, SPAN, 0)
    lane = lax.broadcasted_iota(jnp.int32, (1, LANES), 1)
    lo_head = lane < HEAD_DIM
    zero = jnp.zeros((), BF16)

    for j in range(0, ATTN_QB, SPAN):
        if j == 0:
            mask = band & (kj >= first_lo)
        else:
            mask = band
        for p in range(0, ATTN_WIDTH, LANES):
            q = q_ref[0, j:j + SPAN, p:p + LANES]
            if j == 0:
                k = jnp.concatenate([kp_ref[0, :, p:p + LANES], k_ref[0, 0:SPAN, p:p + LANES]], axis=0)
                v = jnp.concatenate([vp_ref[0, :, p:p + LANES], v_ref[0, 0:SPAN, p:p + LANES]], axis=0)
            else:
                k = k_ref[0, j - SPAN:j + SPAN, p:p + LANES]
                v = v_ref[0, j - SPAN:j + SPAN, p:p + LANES]
            outs, lses = [], []
            for head_lanes in (lo_head, ~lo_head):
                qh = jnp.where(head_lanes, q, zero)
                s = lax.dot_general(qh, k, (((1,), (1,)), ((), ())), preferred_element_type=F32)
                s = jnp.where(mask, s, -jnp.inf)
                m = jnp.max(s, axis=-1, keepdims=True)
                e = jnp.exp(s - m)
                den = jnp.sum(e, axis=-1, keepdims=True)
                outs.append(_dot(e.astype(BF16), v) / den)
                lses.append(m + jnp.log(den))
            o_ref[0, j:j + SPAN, p:p + LANES] = jnp.where(lo_head, outs[0], outs[1]).astype(BF16)
            lse_ref[0, j:j + SPAN, p:p + LANES] = jnp.where(lo_head, lses[0], lses[1])


def _attn_group(qkv, g, dilation):
    b, s, _ = qkv.shape
    length = s // dilation
    assert length % ATTN_QB == 0
    nq = length // ATTN_QB
    sub = ATTN_QB // SPAN
    view = qkv.reshape(b, length, dilation * QKV_COLS)
    per_tok = QKV_COLS // ATTN_WIDTH

    def cur(which):
        return pl.BlockSpec((1, ATTN_QB, ATTN_WIDTH),
                            lambda bi, r, i: (bi, i, r * per_tok + which * N_DIL + g))

    def prev(which):
        return pl.BlockSpec((1, SPAN, ATTN_WIDTH),
                            lambda bi, r, i: (bi, jnp.maximum(i * sub - 1, 0), r * per_tok + which * N_DIL + g))

    out_spec = pl.BlockSpec((1, ATTN_QB, ATTN_WIDTH), lambda bi, r, i: (bi, i, r))
    o, lse = pl.pallas_call(
        _attn_kernel,
        grid=(b, dilation, nq),
        in_specs=[cur(0), cur(1), prev(1), cur(2), prev(2)],
        out_specs=[out_spec, out_spec],
        out_shape=[
            jax.ShapeDtypeStruct((b, length, dilation * ATTN_WIDTH), BF16),
            jax.ShapeDtypeStruct((b, length, dilation * ATTN_WIDTH), F32),
        ],
        compiler_params=pltpu.CompilerParams(
            dimension_semantics=("arbitrary", "arbitrary", "arbitrary"), vmem_limit_bytes=VMEM_LIMIT),
        name=f"attn_d{dilation}",
    )(view, view, view, view, view)
    return o.reshape(b * s, ATTN_WIDTH), lse.reshape(b * s, ATTN_WIDTH)


FF_CHUNKS = ((0, 1024), (1024, 2048), (2048, D_FF))


def _post_kernel(x_ref, o0_ref, o1_ref, o2_ref, l0_ref, l1_ref, l2_ref, sgu_ref, gates_ref,
                 wpa_ref, wps_ref, wo_ref, g2_ref, wg_ref, wu_ref, wd_ref, gf_ref,
                 out_ref, ff_ref):
    l0, l1, l2 = l0_ref[...], l1_ref[...], l2_ref[...]
    mx = jnp.maximum(jnp.maximum(l0, l1), l2)
    e0, e1, e2 = jnp.exp(l0 - mx), jnp.exp(l1 - mx), jnp.exp(l2 - mx)
    inv = 1.0 / (e0 + e1 + e2)
    attn = ((e0 * inv) * o0_ref[...].astype(F32) + (e1 * inv) * o1_ref[...].astype(F32)
            + (e2 * inv) * o2_ref[...].astype(F32)).astype(BF16)

    gate_a = gates_ref[:, 0:D_MODEL].astype(F32)
    gate_b = gates_ref[:, D_MODEL:2 * D_MODEL].astype(F32)
    merged = gate_a * _dot(attn, wpa_ref[...]) + gate_b * _dot(sgu_ref[...], wps_ref[...])
    x1 = x_ref[...] + _dot(merged.astype(BF16), wo_ref[...])

    h2 = (x1 * lax.rsqrt(jnp.mean(x1 * x1, axis=-1, keepdims=True) + EPS) * g2_ref[...]).astype(BF16)
    for c0, c1 in FF_CHUNKS:
        gate = _dot(h2, wg_ref[:, c0:c1])
        up = _dot(h2, wu_ref[:, c0:c1])
        ff_ref[:, c0:c1] = (gate * jax.nn.sigmoid(gate) * up).astype(BF16)
    x2 = x1 + _dot(ff_ref[...], wd_ref[...])
    out_ref[...] = x2 * lax.rsqrt(jnp.mean(x2 * x2, axis=-1, keepdims=True) + EPS) * gf_ref[...]


def _post(x2, os_, lses, sgu, gates, wpa, wps, wo, g2, wg, wu, wd, gf):
    n = x2.shape[0]
    row = lambda width: pl.BlockSpec((POST_TM, width), lambda i: (i, 0))
    return pl.pallas_call(
        _post_kernel,
        grid=(n // POST_TM,),
        in_specs=[
            row(D_MODEL),
            row(ATTN_WIDTH), row(ATTN_WIDTH), row(ATTN_WIDTH),
            row(ATTN_WIDTH), row(ATTN_WIDTH), row(ATTN_WIDTH),
            row(SGU_WIDTH), row(2 * D_MODEL),
            _resident((ATTN_WIDTH, D_MODEL)), _resident((SGU_WIDTH, D_MODEL)),
            _resident((D_MODEL, D_MODEL)), _resident((1, D_MODEL)),
            _resident((D_MODEL, D_FF)), _resident((D_MODEL, D_FF)), _resident((D_FF, D_MODEL)),
            _resident((1, D_MODEL)),
        ],
        out_specs=row(D_MODEL),
        out_shape=jax.ShapeDtypeStruct((n, D_MODEL), F32),
        scratch_shapes=[pltpu.VMEM((POST_TM, D_FF), BF16)],
        compiler_params=pltpu.CompilerParams(
            dimension_semantics=("arbitrary",), vmem_limit_bytes=VMEM_LIMIT),
        name="post",
    )(x2, *os_, *lses, sgu, gates, wpa, wps, wo, g2, wg, wu, wd, gf)


def _rope_freq_row():
    inv_freq = ROPE_THETA ** (-jnp.arange(0, ROPE_DIM, 2, dtype=F32) / ROPE_DIM)
    per_head = jnp.concatenate([inv_freq, inv_freq, jnp.zeros((HEAD_DIM - ROPE_DIM,), F32)])
    return jnp.tile(per_head, LANES // HEAD_DIM)[None, :]


def kernel(x, positions, norm1_g, w_in, sgu_ln_g, sgu_ln_b, w_spatial, b_spatial, w_proj_attn,
           w_proj_sgu, w_out, norm2_g, w_ffn_gate, w_ffn_up, w_ffn_down, final_g):
    b, s, d = x.shape
    depth = w_in.shape[0]
    x2 = x.reshape(b * s, d)
    pos2 = positions.reshape(b * s, 1)
    freq = _rope_freq_row()
    for l in range(depth):
        ws_all = jnp.transpose(w_spatial[l], (1, 0, 2)).reshape(SGU_CHUNK, SGU_GROUPS * SGU_CHUNK)
        bs_full = jnp.repeat(jnp.transpose(b_spatial[l]), SGU_GROUP_DIM, axis=1)
        qkv, sgu, gates = _in_proj(
            x2, pos2, norm1_g[l][None, :], w_in[l].astype(BF16), freq,
            sgu_ln_g[l][None, :], sgu_ln_b[l][None, :], ws_all.astype(BF16), bs_full)
        qkv3 = qkv.reshape(b, s, QKV_COLS)
        os_, lses = [], []
        for g, dilation in enumerate(GROUP_DILATIONS):
            o, lse = _attn_group(qkv3, g, dilation)
            os_.append(o)
            lses.append(lse)
        gf = final_g[None, :] if l == depth - 1 else None
        assert gf is not None, "only the last layer fuses the final norm"
        x2 = _post(x2, os_, lses, sgu, gates,
                   w_proj_attn[l].astype(BF16), w_proj_sgu[l].astype(BF16), w_out[l].astype(BF16),
                   norm2_g[l][None, :], w_ffn_gate[l].astype(BF16), w_ffn_up[l].astype(BF16),
                   w_ffn_down[l].astype(BF16), gf)
    return x2.reshape(b, s, d)
```

```python
import jax
import jax.numpy as jnp
import numpy as np
from jax import lax
from jax.experimental import pallas as pl
from jax.experimental.pallas import tpu as pltpu

D_MODEL = 1024
HEAD_DIM = 64
HEADS = 8
GROUP_DILATIONS = (1, 4, 16)
N_DIL = len(GROUP_DILATIONS)
SPAN = 128
ATTN_WIDTH = HEADS * HEAD_DIM
ROPE_DIM = HEAD_DIM // 4
ROPE_HALF = ROPE_DIM // 2
ROPE_THETA = 500000.0
SGU_CHUNK = 128
SGU_GROUPS = 8
SGU_WIDTH = D_MODEL // 2
SGU_GROUP_DIM = SGU_WIDTH // SGU_GROUPS
D_FF = 2816
QKV_COLS = 3 * N_DIL * ATTN_WIDTH
UV_OFF = QKV_COLS
GATE_OFF = QKV_COLS + 2 * SGU_WIDTH
IN_COLS = GATE_OFF + 2 * D_MODEL
EPS = 1e-6

LANES = 128
V7X_VMEM_BYTES = 64 * 1024 * 1024
VMEM_LIMIT = V7X_VMEM_BYTES - 8 * 1024 * 1024

IN_TM = 256
POST_TM = 256
ATTN_QB = 512
COL_CHUNK = 512

BF16 = jnp.bfloat16
F32 = jnp.float32


def _resident(shape):
    nd = len(shape)
    return pl.BlockSpec(shape, lambda *_: (0,) * nd, pipeline_mode=pl.Buffered(1))


def _dot(a, b):
    return jnp.dot(a, b, preferred_element_type=F32)


def _in_proj_kernel(x_ref, pos_ref, g1_ref, w_ref, freq_ref, lng_ref, lnb_ref, ws_ref, bs_ref,
                    qkv0_ref, qkv1_ref, qkv2_ref, sgu_ref, gates_ref, st_ref):
    tm = x_ref.shape[0]
    x = x_ref[...]
    h = (x * lax.rsqrt(jnp.mean(x * x, axis=-1, keepdims=True) + EPS) * g1_ref[...]).astype(BF16)

    ang = pos_ref[...].astype(F32) * freq_ref[...]
    cos = jnp.cos(ang)
    sin = jnp.sin(ang)
    lane = lax.broadcasted_iota(jnp.int32, (1, LANES), 1) % HEAD_DIM
    sin_up = jnp.where((lane >= ROPE_HALF) & (lane < ROPE_DIM), sin, 0.0)
    sin_dn = jnp.where(lane < ROPE_HALF, -sin, 0.0)

    qkv_refs = (qkv0_ref, qkv1_ref, qkv2_ref)
    for which in range(3):
        for g, dilation in enumerate(GROUP_DILATIONS):
            c = (which * N_DIL + g) * ATTN_WIDTH
            y = _dot(h, w_ref[:, c:c + ATTN_WIDTH])
            out_ref = qkv_refs[g]
            o0 = which * ATTN_WIDTH
            if which < 2:
                scale = HEAD_DIM ** -0.5 if which == 0 else 1.0
                for l in range(0, ATTN_WIDTH, LANES):
                    t = y[:, l:l + LANES]
                    t = (t * cos + pltpu.roll(t, ROPE_HALF, 1) * sin_up
                         + pltpu.roll(t, LANES - ROPE_HALF, 1) * sin_dn) * scale
                    if dilation == 1:
                        out_ref[0, :, o0 + l:o0 + l + LANES] = t.astype(BF16)
                    else:
                        st_ref[l // LANES] = t
            elif dilation == 1:
                out_ref[0, :, o0:o0 + ATTN_WIDTH] = y.astype(BF16)
            else:
                for l in range(0, ATTN_WIDTH, LANES):
                    st_ref[l // LANES] = y[:, l:l + LANES]
            if dilation > 1:
                for r in range(dilation):
                    for l in range(0, ATTN_WIDTH, LANES):
                        out_ref[r, :, o0 + l:o0 + l + LANES] = st_ref[
                            l // LANES, pl.ds(r, tm // dilation, stride=dilation), :].astype(BF16)

    def gelu(y):
        return 0.5 * y * (1.0 + lax.erf(y * np.float32(np.sqrt(0.5))))

    u = gelu(_dot(h, w_ref[:, UV_OFF:UV_OFF + SGU_WIDTH]))
    v = gelu(_dot(h, w_ref[:, UV_OFF + SGU_WIDTH:UV_OFF + 2 * SGU_WIDTH]))
    mu = jnp.mean(v, axis=-1, keepdims=True)
    vc = v - mu
    v = vc * lax.rsqrt(jnp.mean(vc * vc, axis=-1, keepdims=True) + EPS) * lng_ref[...] + lnb_ref[...]
    v = v.astype(BF16)

    row = lax.broadcasted_iota(jnp.int32, (SGU_CHUNK, SGU_GROUPS * SGU_CHUNK), 0)
    col = lax.broadcasted_iota(jnp.int32, (SGU_CHUNK, SGU_GROUPS * SGU_CHUNK), 1) % SGU_CHUNK
    w_causal = jnp.where(col <= row, ws_ref[...], jnp.zeros((), BF16))
    lane_group = lax.broadcasted_iota(jnp.int32, (1, SGU_WIDTH), 1) // SGU_GROUP_DIM
    for r0 in range(0, tm, SGU_CHUNK):
        vch = v[r0:r0 + SGU_CHUNK]
        v_bd = jnp.concatenate(
            [jnp.where(lane_group == g, vch, jnp.zeros((), BF16)) for g in range(SGU_GROUPS)], axis=0)
        mixed = _dot(w_causal, v_bd) + bs_ref[...]
        sgu_ref[r0:r0 + SGU_CHUNK, :] = (u[r0:r0 + SGU_CHUNK] * mixed).astype(BF16)

    for c in range(GATE_OFF, IN_COLS, COL_CHUNK):
        y = _dot(h, w_ref[:, c:c + COL_CHUNK])
        gates_ref[:, c - GATE_OFF:c - GATE_OFF + COL_CHUNK] = jax.nn.sigmoid(y).astype(BF16)


def _in_proj(x2, pos2, g1, w_in, freq, lng, lnb, ws_all, bs_full, batch):
    n = x2.shape[0]
    seq = n // batch
    tiles = seq // IN_TM
    row = lambda width: pl.BlockSpec((IN_TM, width), lambda i: (i, 0))

    def strided(d):
        return pl.BlockSpec((None, d, IN_TM // d, 3 * ATTN_WIDTH),
                            lambda i: (i // tiles, 0, i % tiles, 0))

    return pl.pallas_call(
        _in_proj_kernel,
        grid=(n // IN_TM,),
        in_specs=[
            row(D_MODEL), row(1),
            _resident((1, D_MODEL)), _resident((D_MODEL, IN_COLS)), _resident((1, LANES)),
            _resident((1, SGU_WIDTH)), _resident((1, SGU_WIDTH)),
            _resident((SGU_CHUNK, SGU_GROUPS * SGU_CHUNK)), _resident((SGU_CHUNK, SGU_WIDTH)),
        ],
        out_specs=[strided(d) for d in GROUP_DILATIONS] + [row(SGU_WIDTH), row(2 * D_MODEL)],
        out_shape=[jax.ShapeDtypeStruct((batch, d, seq // d, 3 * ATTN_WIDTH), BF16)
                   for d in GROUP_DILATIONS] + [
            jax.ShapeDtypeStruct((n, SGU_WIDTH), BF16),
            jax.ShapeDtypeStruct((n, 2 * D_MODEL), BF16),
        ],
        scratch_shapes=[pltpu.VMEM((ATTN_WIDTH // LANES, IN_TM, LANES), F32)],
        compiler_params=pltpu.CompilerParams(
            dimension_semantics=("arbitrary",), vmem_limit_bytes=VMEM_LIMIT),
        name="in_proj",
    )(x2, pos2, g1, w_in, freq, lng, lnb, ws_all, bs_full)


def _attn_kernel(q_ref, k_ref, kp_ref, v_ref, vp_ref, o_ref, lse_ref):
    first = pl.program_id(1) == 0
    qi = lax.broadcasted_iota(jnp.int32, (SPAN, 2 * SPAN), 0)
    kj = lax.broadcasted_iota(jnp.int32, (SPAN, 2 * SPAN), 1)
    band = (kj >= qi) & (kj <= qi + SPAN)
    first_lo = jnp.where(first, SPAN, 0)
    lane = lax.broadcasted_iota(jnp.int32, (1, LANES), 1)
    lo_head = lane < HEAD_DIM
    zero = jnp.zeros((), BF16)

    for j in range(0, ATTN_QB, SPAN):
        if j == 0:
            mask = band & (kj >= first_lo)
        else:
            mask = band
        for p in range(0, ATTN_WIDTH, LANES):
            q = q_ref[0, j:j + SPAN, p:p + LANES]
            if j == 0:
                k = jnp.concatenate([kp_ref[0, :, p:p + LANES], k_ref[0, 0:SPAN, p:p + LANES]], axis=0)
                v = jnp.concatenate([vp_ref[0, :, p:p + LANES], v_ref[0, 0:SPAN, p:p + LANES]], axis=0)
            else:
                k = k_ref[0, j - SPAN:j + SPAN, p:p + LANES]
                v = v_ref[0, j - SPAN:j + SPAN, p:p + LANES]
            outs, lses = [], []
            for head_lanes in (lo_head, ~lo_head):
                qh = jnp.where(head_lanes, q, zero)
                s = lax.dot_general(qh, k, (((1,), (1,)), ((), ())), preferred_element_type=F32)
                s = jnp.where(mask, s, -jnp.inf)
                m = jnp.max(s, axis=-1, keepdims=True)
                e = jnp.exp(s - m)
                den = jnp.sum(e, axis=-1, keepdims=True)
                outs.append(_dot(e.astype(BF16), v) / den)
                lses.append(m + jnp.log(den))
            o_ref[0, j:j + SPAN, p:p + LANES] = jnp.where(lo_head, outs[0], outs[1]).astype(BF16)
            lse_ref[0, j:j + SPAN, p:p + LANES] = jnp.where(lo_head, lses[0], lses[1])


def _attn_group(qkv):
    rows, length, _ = qkv.shape
    assert length % ATTN_QB == 0
    nq = length // ATTN_QB
    sub = ATTN_QB // SPAN

    def cur(which):
        return pl.BlockSpec((1, ATTN_QB, ATTN_WIDTH), lambda r, i: (r, i, which))

    def prev(which):
        return pl.BlockSpec((1, SPAN, ATTN_WIDTH),
                            lambda r, i: (r, jnp.maximum(i * sub - 1, 0), which))

    out_spec = pl.BlockSpec((1, ATTN_QB, ATTN_WIDTH), lambda r, i: (r, i, 0))
    return pl.pallas_call(
        _attn_kernel,
        grid=(rows, nq),
        in_specs=[cur(0), cur(1), prev(1), cur(2), prev(2)],
        out_specs=[out_spec, out_spec],
        out_shape=[
            jax.ShapeDtypeStruct((rows, length, ATTN_WIDTH), BF16),
            jax.ShapeDtypeStruct((rows, length, ATTN_WIDTH), F32),
        ],
        compiler_params=pltpu.CompilerParams(
            dimension_semantics=("arbitrary", "arbitrary"), vmem_limit_bytes=VMEM_LIMIT),
        name=f"attn_len{length}",
    )(qkv, qkv, qkv, qkv, qkv)


FF_CHUNKS = ((0, 1024), (1024, 2048), (2048, D_FF))


def _post_kernel(x_ref, o0_ref, o1_ref, o2_ref, l0_ref, l1_ref, l2_ref, sgu_ref, gates_ref,
                 wpa_ref, wps_ref, wo_ref, g2_ref, wg_ref, wu_ref, wd_ref, gf_ref,
                 out_ref, ff_ref, tok_refs):
    tm = x_ref.shape[0]

    def token_major(ref, scratch):
        dilation = ref.shape[0]
        if dilation == 1:
            return ref[0].astype(F32)
        for r in range(dilation):
            for l in range(0, ATTN_WIDTH, LANES):
                scratch[l // LANES, pl.ds(r, tm // dilation, stride=dilation), :] = (
                    ref[r, :, l:l + LANES].astype(F32))
        return jnp.concatenate([scratch[i] for i in range(ATTN_WIDTH // LANES)], axis=1)

    l0 = token_major(l0_ref, None)
    l1 = token_major(l1_ref, tok_refs.at[0])
    l2 = token_major(l2_ref, tok_refs.at[1])
    mx = jnp.maximum(jnp.maximum(l0, l1), l2)
    e0, e1, e2 = jnp.exp(l0 - mx), jnp.exp(l1 - mx), jnp.exp(l2 - mx)
    inv = 1.0 / (e0 + e1 + e2)
    attn = ((e0 * inv) * token_major(o0_ref, None)
            + (e1 * inv) * token_major(o1_ref, tok_refs.at[2])
            + (e2 * inv) * token_major(o2_ref, tok_refs.at[3])).astype(BF16)

    gate_a = gates_ref[:, 0:D_MODEL].astype(F32)
    gate_b = gates_ref[:, D_MODEL:2 * D_MODEL].astype(F32)
    merged = gate_a * _dot(attn, wpa_ref[...]) + gate_b * _dot(sgu_ref[...], wps_ref[...])
    x1 = x_ref[...] + _dot(merged.astype(BF16), wo_ref[...])

    h2 = (x1 * lax.rsqrt(jnp.mean(x1 * x1, axis=-1, keepdims=True) + EPS) * g2_ref[...]).astype(BF16)
    for c0, c1 in FF_CHUNKS:
        gate = _dot(h2, wg_ref[:, c0:c1])
        up = _dot(h2, wu_ref[:, c0:c1])
        ff_ref[:, c0:c1] = (gate * jax.nn.sigmoid(gate) * up).astype(BF16)
    x2 = x1 + _dot(ff_ref[...], wd_ref[...])
    out_ref[...] = x2 * lax.rsqrt(jnp.mean(x2 * x2, axis=-1, keepdims=True) + EPS) * gf_ref[...]


def _post(x2, os_, lses, sgu, gates, wpa, wps, wo, g2, wg, wu, wd, gf, batch):
    n = x2.shape[0]
    seq = n // batch
    tiles = seq // POST_TM
    row = lambda width: pl.BlockSpec((POST_TM, width), lambda i: (i, 0))

    def strided(d):
        return pl.BlockSpec((None, d, POST_TM // d, ATTN_WIDTH),
                            lambda i: (i // tiles, 0, i % tiles, 0))

    group_specs = [strided(d) for d in GROUP_DILATIONS]
    return pl.pallas_call(
        _post_kernel,
        grid=(n // POST_TM,),
        in_specs=[row(D_MODEL)] + group_specs + group_specs + [
            row(SGU_WIDTH), row(2 * D_MODEL),
            _resident((ATTN_WIDTH, D_MODEL)), _resident((SGU_WIDTH, D_MODEL)),
            _resident((D_MODEL, D_MODEL)), _resident((1, D_MODEL)),
            _resident((D_MODEL, D_FF)), _resident((D_MODEL, D_FF)), _resident((D_FF, D_MODEL)),
            _resident((1, D_MODEL)),
        ],
        out_specs=row(D_MODEL),
        out_shape=jax.ShapeDtypeStruct((n, D_MODEL), F32),
        scratch_shapes=[pltpu.VMEM((POST_TM, D_FF), BF16),
                        pltpu.VMEM((4, ATTN_WIDTH // LANES, POST_TM, LANES), F32)],
        compiler_params=pltpu.CompilerParams(
            dimension_semantics=("arbitrary",), vmem_limit_bytes=VMEM_LIMIT),
        name="post",
    )(x2, *os_, *lses, sgu, gates, wpa, wps, wo, g2, wg, wu, wd, gf)


def _rope_freq_row():
    inv_freq = ROPE_THETA ** (-jnp.arange(0, ROPE_DIM, 2, dtype=F32) / ROPE_DIM)
    per_head = jnp.concatenate([inv_freq, inv_freq, jnp.zeros((HEAD_DIM - ROPE_DIM,), F32)])
    return jnp.tile(per_head, LANES // HEAD_DIM)[None, :]


def kernel(x, positions, norm1_g, w_in, sgu_ln_g, sgu_ln_b, w_spatial, b_spatial, w_proj_attn,
           w_proj_sgu, w_out, norm2_g, w_ffn_gate, w_ffn_up, w_ffn_down, final_g):
    b, s, d = x.shape
    assert w_in.shape[0] == 1, "single-layer block"
    x2 = x.reshape(b * s, d)
    pos2 = positions.reshape(b * s, 1)
    ws_all = jnp.transpose(w_spatial[0], (1, 0, 2)).reshape(SGU_CHUNK, SGU_GROUPS * SGU_CHUNK)
    bs_full = jnp.repeat(jnp.transpose(b_spatial[0]), SGU_GROUP_DIM, axis=1)
    *qkvs, sgu, gates = _in_proj(
        x2, pos2, norm1_g[0][None, :], w_in[0].astype(BF16), _rope_freq_row(),
        sgu_ln_g[0][None, :], sgu_ln_b[0][None, :], ws_all.astype(BF16), bs_full, b)
    os_, lses = [], []
    for qkv, dilation in zip(qkvs, GROUP_DILATIONS):
        length = s // dilation
        o, lse = _attn_group(qkv.reshape(b * dilation, length, 3 * ATTN_WIDTH))
        os_.append(o.reshape(b, dilation, length, ATTN_WIDTH))
        lses.append(lse.reshape(b, dilation, length, ATTN_WIDTH))
    out = _post(x2, os_, lses, sgu, gates,
                w_proj_attn[0].astype(BF16), w_proj_sgu[0].astype(BF16), w_out[0].astype(BF16),
                norm2_g[0][None, :], w_ffn_gate[0].astype(BF16), w_ffn_up[0].astype(BF16),
                w_ffn_down[0].astype(BF16), final_g[None, :], b)
    return out.reshape(b, s, d)
```

```python
import jax
import jax.numpy as jnp
import numpy as np
from jax import lax
from jax.experimental import pallas as pl
from jax.experimental.pallas import tpu as pltpu

D_MODEL = 1024
HEAD_DIM = 64
HEADS = 8
GROUP_DILATIONS = (1, 4, 16)
N_DIL = len(GROUP_DILATIONS)
SPAN = 128
ATTN_WIDTH = HEADS * HEAD_DIM
ROPE_DIM = HEAD_DIM // 4
ROPE_HALF = ROPE_DIM // 2
ROPE_THETA = 500000.0
SGU_CHUNK = 128
SGU_GROUPS = 8
SGU_WIDTH = D_MODEL // 2
SGU_GROUP_DIM = SGU_WIDTH // SGU_GROUPS
D_FF = 2816
QKV_COLS = 3 * N_DIL * ATTN_WIDTH
UV_OFF = QKV_COLS
GATE_OFF = QKV_COLS + 2 * SGU_WIDTH
IN_COLS = GATE_OFF + 2 * D_MODEL
EPS = 1e-6

LANES = 128
V7X_VMEM_BYTES = 64 * 1024 * 1024
VMEM_LIMIT = V7X_VMEM_BYTES - 8 * 1024 * 1024

IN_TM = 512
POST_TM = 512
ATTN_QB = 512
COL_CHUNK = 512

BF16 = jnp.bfloat16
F32 = jnp.float32


def _resident(shape):
    nd = len(shape)
    return pl.BlockSpec(shape, lambda *_: (0,) * nd, pipeline_mode=pl.Buffered(1))


def _dot(a, b):
    return jnp.dot(a, b, preferred_element_type=F32)


def _in_proj_kernel(x_ref, pos_ref, g1_ref, w_ref, freq_ref, lng_ref, lnb_ref, ws_ref, bs_ref,
                    qkv0_ref, qkv1_ref, qkv2_ref, sgu_ref, gates_ref, st_ref):
    tm = x_ref.shape[0]
    x = x_ref[...]
    h = (x * lax.rsqrt(jnp.mean(x * x, axis=-1, keepdims=True) + EPS) * g1_ref[...]).astype(BF16)

    ang = pos_ref[...].astype(F32) * freq_ref[...]
    cos = jnp.cos(ang)
    sin = jnp.sin(ang)
    lane = lax.broadcasted_iota(jnp.int32, (1, LANES), 1) % HEAD_DIM
    sin_up = jnp.where((lane >= ROPE_HALF) & (lane < ROPE_DIM), sin, 0.0)
    sin_dn = jnp.where(lane < ROPE_HALF, -sin, 0.0)

    qkv_refs = (qkv0_ref, qkv1_ref, qkv2_ref)
    for which in range(3):
        for g, dilation in enumerate(GROUP_DILATIONS):
            c = (which * N_DIL + g) * ATTN_WIDTH
            y = _dot(h, w_ref[:, c:c + ATTN_WIDTH])
            out_ref = qkv_refs[g]
            o0 = which * ATTN_WIDTH
            if which < 2:
                scale = HEAD_DIM ** -0.5 if which == 0 else 1.0
                for l in range(0, ATTN_WIDTH, LANES):
                    t = y[:, l:l + LANES]
                    t = (t * cos + pltpu.roll(t, ROPE_HALF, 1) * sin_up
                         + pltpu.roll(t, LANES - ROPE_HALF, 1) * sin_dn) * scale
                    if dilation == 1:
                        out_ref[0, :, o0 + l:o0 + l + LANES] = t.astype(BF16)
                    else:
                        st_ref[l // LANES] = t
            elif dilation == 1:
                out_ref[0, :, o0:o0 + ATTN_WIDTH] = y.astype(BF16)
            else:
                for l in range(0, ATTN_WIDTH, LANES):
                    st_ref[l // LANES] = y[:, l:l + LANES]
            if dilation > 1:
                for r in range(dilation):
                    for l in range(0, ATTN_WIDTH, LANES):
                        out_ref[r, :, o0 + l:o0 + l + LANES] = st_ref[
                            l // LANES, pl.ds(r, tm // dilation, stride=dilation), :].astype(BF16)

    def gelu(y):
        return 0.5 * y * (1.0 + lax.erf(y * np.float32(np.sqrt(0.5))))

    u = gelu(_dot(h, w_ref[:, UV_OFF:UV_OFF + SGU_WIDTH]))
    v = gelu(_dot(h, w_ref[:, UV_OFF + SGU_WIDTH:UV_OFF + 2 * SGU_WIDTH]))
    mu = jnp.mean(v, axis=-1, keepdims=True)
    vc = v - mu
    v = vc * lax.rsqrt(jnp.mean(vc * vc, axis=-1, keepdims=True) + EPS) * lng_ref[...] + lnb_ref[...]
    v = v.astype(BF16)

    row = lax.broadcasted_iota(jnp.int32, (SGU_CHUNK, SGU_GROUPS * SGU_CHUNK), 0)
    col = lax.broadcasted_iota(jnp.int32, (SGU_CHUNK, SGU_GROUPS * SGU_CHUNK), 1) % SGU_CHUNK
    w_causal = jnp.where(col <= row, ws_ref[...], jnp.zeros((), BF16))
    lane_group = lax.broadcasted_iota(jnp.int32, (1, SGU_WIDTH), 1) // SGU_GROUP_DIM
    for r0 in range(0, tm, SGU_CHUNK):
        vch = v[r0:r0 + SGU_CHUNK]
        v_bd = jnp.concatenate(
            [jnp.where(lane_group == g, vch, jnp.zeros((), BF16)) for g in range(SGU_GROUPS)], axis=0)
        mixed = _dot(w_causal, v_bd) + bs_ref[...]
        sgu_ref[r0:r0 + SGU_CHUNK, :] = (u[r0:r0 + SGU_CHUNK] * mixed).astype(BF16)

    for c in range(GATE_OFF, IN_COLS, COL_CHUNK):
        y = _dot(h, w_ref[:, c:c + COL_CHUNK])
        gates_ref[:, c - GATE_OFF:c - GATE_OFF + COL_CHUNK] = jax.nn.sigmoid(y).astype(BF16)


def _in_proj(x2, pos2, g1, w_in, freq, lng, lnb, ws_all, bs_full, batch):
    n = x2.shape[0]
    seq = n // batch
    tiles = seq // IN_TM
    row = lambda width: pl.BlockSpec((IN_TM, width), lambda i: (i, 0))

    def strided(d):
        return pl.BlockSpec((None, d, IN_TM // d, 3 * ATTN_WIDTH),
                            lambda i: (i // tiles, 0, i % tiles, 0))

    return pl.pallas_call(
        _in_proj_kernel,
        grid=(n // IN_TM,),
        in_specs=[
            row(D_MODEL), row(1),
            _resident((1, D_MODEL)), _resident((D_MODEL, IN_COLS)), _resident((1, LANES)),
            _resident((1, SGU_WIDTH)), _resident((1, SGU_WIDTH)),
            _resident((SGU_CHUNK, SGU_GROUPS * SGU_CHUNK)), _resident((SGU_CHUNK, SGU_WIDTH)),
        ],
        out_specs=[strided(d) for d in GROUP_DILATIONS] + [row(SGU_WIDTH), row(2 * D_MODEL)],
        out_shape=[jax.ShapeDtypeStruct((batch, d, seq // d, 3 * ATTN_WIDTH), BF16)
                   for d in GROUP_DILATIONS] + [
            jax.ShapeDtypeStruct((n, SGU_WIDTH), BF16),
            jax.ShapeDtypeStruct((n, 2 * D_MODEL), BF16),
        ],
        scratch_shapes=[pltpu.VMEM((ATTN_WIDTH // LANES, IN_TM, LANES), F32)],
        compiler_params=pltpu.CompilerParams(
            dimension_semantics=("arbitrary",), vmem_limit_bytes=VMEM_LIMIT),
        name="in_proj",
    )(x2, pos2, g1, w_in, freq, lng, lnb, ws_all, bs_full)


def _attn_kernel(q_ref, k_ref, kp_ref, v_ref, vp_ref, o_ref, lse_ref):
    first = pl.program_id(1) == 0
    qi = lax.broadcasted_iota(jnp.int32, (SPAN, 2 * SPAN), 0)
    kj = lax.broadcasted_iota(jnp.int32, (SPAN, 2 * SPAN), 1)
    band = (kj >= qi) & (kj <= qi + SPAN)
    first_lo = jnp.where(first, SPAN, 0)
    lane = lax.broadcasted_iota(jnp.int32, (1, LANES), 1)
    lo_head = lane < HEAD_DIM
    zero = jnp.zeros((), BF16)

    for j in range(0, ATTN_QB, SPAN):
        if j == 0:
            mask = band & (kj >= first_lo)
        else:
            mask = band
        lse_blk = jnp.zeros((SPAN, LANES), F32)
        for p in range(0, ATTN_WIDTH, LANES):
            q = q_ref[0, j:j + SPAN, p:p + LANES]
            if j == 0:
                k = jnp.concatenate([kp_ref[0, :, p:p + LANES], k_ref[0, 0:SPAN, p:p + LANES]], axis=0)
                v = jnp.concatenate([vp_ref[0, :, p:p + LANES], v_ref[0, 0:SPAN, p:p + LANES]], axis=0)
            else:
                k = k_ref[0, j - SPAN:j + SPAN, p:p + LANES]
                v = v_ref[0, j - SPAN:j + SPAN, p:p + LANES]
            outs = []
            for half, head_lanes in enumerate((lo_head, ~lo_head)):
                qh = jnp.where(head_lanes, q, zero)
                s = lax.dot_general(qh, k, (((1,), (1,)), ((), ())), preferred_element_type=F32)
                s = jnp.where(mask, s, -jnp.inf)
                m = jnp.max(s, axis=-1, keepdims=True)
                e = jnp.exp(s - m)
                den = jnp.sum(e, axis=-1, keepdims=True)
                outs.append(_dot(e.astype(BF16), v) / den)
                head = p // HEAD_DIM + half
                lse_blk = jnp.where(lane == head, m + jnp.log(den), lse_blk)
            o_ref[0, j:j + SPAN, p:p + LANES] = jnp.where(lo_head, outs[0], outs[1]).astype(BF16)
        lse_ref[0, j:j + SPAN, :] = lse_blk


def _attn_group(qkv):
    rows, length, _ = qkv.shape
    assert length % ATTN_QB == 0
    nq = length // ATTN_QB
    sub = ATTN_QB // SPAN

    def cur(which):
        return pl.BlockSpec((1, ATTN_QB, ATTN_WIDTH), lambda r, i: (r, i, which))

    def prev(which):
        return pl.BlockSpec((1, SPAN, ATTN_WIDTH),
                            lambda r, i: (r, jnp.maximum(i * sub - 1, 0), which))

    out_spec = lambda width: pl.BlockSpec((1, ATTN_QB, width), lambda r, i: (r, i, 0))
    return pl.pallas_call(
        _attn_kernel,
        grid=(rows, nq),
        in_specs=[cur(0), cur(1), prev(1), cur(2), prev(2)],
        out_specs=[out_spec(ATTN_WIDTH), out_spec(LANES)],
        out_shape=[
            jax.ShapeDtypeStruct((rows, length, ATTN_WIDTH), BF16),
            jax.ShapeDtypeStruct((rows, length, LANES), F32),
        ],
        compiler_params=pltpu.CompilerParams(
            dimension_semantics=("arbitrary", "arbitrary"), vmem_limit_bytes=VMEM_LIMIT),
        name=f"attn_len{length}",
    )(qkv, qkv, qkv, qkv, qkv)


FF_CHUNKS = ((0, 1024), (1024, 2048), (2048, D_FF))


def _post_kernel(x_ref, o0_ref, o1_ref, o2_ref, l0_ref, l1_ref, l2_ref, sgu_ref, gates_ref,
                 wpa_ref, wps_ref, wo_ref, g2_ref, wg_ref, wu_ref, wd_ref, gf_ref, ex_ref,
                 out_ref, ff_ref, tok_l, tok_o):
    tm = x_ref.shape[0]

    def token_major(ref, scratch):
        dilation, _, width = ref.shape
        if dilation == 1:
            return ref[0].astype(F32)
        for r in range(dilation):
            for l in range(0, width, LANES):
                scratch[l // LANES, pl.ds(r, tm // dilation, stride=dilation), :] = (
                    ref[r, :, l:l + LANES].astype(F32))
        return jnp.concatenate([scratch[i] for i in range(width // LANES)], axis=1)

    def per_head_to_lanes(a):
        hi = a.astype(BF16)
        lo = (a - hi.astype(F32)).astype(BF16)
        return _dot(jnp.concatenate([hi, lo], axis=1), ex_ref[...])

    l0 = token_major(l0_ref, None)
    l1 = token_major(l1_ref, tok_l.at[0])
    l2 = token_major(l2_ref, tok_l.at[1])
    mx = jnp.maximum(jnp.maximum(l0, l1), l2)
    e0, e1, e2 = jnp.exp(l0 - mx), jnp.exp(l1 - mx), jnp.exp(l2 - mx)
    inv = 1.0 / (e0 + e1 + e2)
    attn = (per_head_to_lanes(e0 * inv) * token_major(o0_ref, None)
            + per_head_to_lanes(e1 * inv) * token_major(o1_ref, tok_o.at[0])
            + per_head_to_lanes(e2 * inv) * token_major(o2_ref, tok_o.at[1])).astype(BF16)

    gate_a = gates_ref[:, 0:D_MODEL].astype(F32)
    gate_b = gates_ref[:, D_MODEL:2 * D_MODEL].astype(F32)
    merged = gate_a * _dot(attn, wpa_ref[...]) + gate_b * _dot(sgu_ref[...], wps_ref[...])
    x1 = x_ref[...] + _dot(merged.astype(BF16), wo_ref[...])

    h2 = (x1 * lax.rsqrt(jnp.mean(x1 * x1, axis=-1, keepdims=True) + EPS) * g2_ref[...]).astype(BF16)
    for c0, c1 in FF_CHUNKS:
        gate = _dot(h2, wg_ref[:, c0:c1])
        up = _dot(h2, wu_ref[:, c0:c1])
        ff_ref[:, c0:c1] = (gate * jax.nn.sigmoid(gate) * up).astype(BF16)
    x2 = x1 + _dot(ff_ref[...], wd_ref[...])
    out_ref[...] = x2 * lax.rsqrt(jnp.mean(x2 * x2, axis=-1, keepdims=True) + EPS) * gf_ref[...]


def _post(x2, os_, lses, sgu, gates, wpa, wps, wo, g2, wg, wu, wd, gf, batch):
    n = x2.shape[0]
    seq = n // batch
    tiles = seq // POST_TM
    row = lambda width: pl.BlockSpec((POST_TM, width), lambda i: (i, 0))

    def strided(d, width):
        return pl.BlockSpec((None, d, POST_TM // d, width),
                            lambda i: (i // tiles, 0, i % tiles, 0))

    head_of_lane = np.arange(ATTN_WIDTH) // HEAD_DIM
    expand = (np.arange(2 * LANES)[:, None] % LANES == head_of_lane[None, :])
    expand = jnp.asarray(expand, dtype=BF16)

    return pl.pallas_call(
        _post_kernel,
        grid=(n // POST_TM,),
        in_specs=[row(D_MODEL)] + [strided(d, ATTN_WIDTH) for d in GROUP_DILATIONS]
        + [strided(d, LANES) for d in GROUP_DILATIONS] + [
            row(SGU_WIDTH), row(2 * D_MODEL),
            _resident((ATTN_WIDTH, D_MODEL)), _resident((SGU_WIDTH, D_MODEL)),
            _resident((D_MODEL, D_MODEL)), _resident((1, D_MODEL)),
            _resident((D_MODEL, D_FF)), _resident((D_MODEL, D_FF)), _resident((D_FF, D_MODEL)),
            _resident((1, D_MODEL)), _resident((2 * LANES, ATTN_WIDTH)),
        ],
        out_specs=row(D_MODEL),
        out_shape=jax.ShapeDtypeStruct((n, D_MODEL), F32),
        scratch_shapes=[pltpu.VMEM((POST_TM, D_FF), BF16),
                        pltpu.VMEM((2, 1, POST_TM, LANES), F32),
                        pltpu.VMEM((2, ATTN_WIDTH // LANES, POST_TM, LANES), F32)],
        compiler_params=pltpu.CompilerParams(
            dimension_semantics=("arbitrary",), vmem_limit_bytes=VMEM_LIMIT),
        name="post",
    )(x2, *os_, *lses, sgu, gates, wpa, wps, wo, g2, wg, wu, wd, gf, expand)


def _rope_freq_row():
    inv_freq = ROPE_THETA ** (-jnp.arange(0, ROPE_DIM, 2, dtype=F32) / ROPE_DIM)
    per_head = jnp.concatenate([inv_freq, inv_freq, jnp.zeros((HEAD_DIM - ROPE_DIM,), F32)])
    return jnp.tile(per_head, LANES // HEAD_DIM)[None, :]


def kernel(x, positions, norm1_g, w_in, sgu_ln_g, sgu_ln_b, w_spatial, b_spatial, w_proj_attn,
           w_proj_sgu, w_out, norm2_g, w_ffn_gate, w_ffn_up, w_ffn_down, final_g):
    b, s, d = x.shape
    assert w_in.shape[0] == 1, "single-layer block"
    x2 = x.reshape(b * s, d)
    pos2 = positions.reshape(b * s, 1)
    ws_all = jnp.transpose(w_spatial[0], (1, 0, 2)).reshape(SGU_CHUNK, SGU_GROUPS * SGU_CHUNK)
    bs_full = jnp.repeat(jnp.transpose(b_spatial[0]), SGU_GROUP_DIM, axis=1)
    *qkvs, sgu, gates = _in_proj(
        x2, pos2, norm1_g[0][None, :], w_in[0].astype(BF16), _rope_freq_row(),
        sgu_ln_g[0][None, :], sgu_ln_b[0][None, :], ws_all.astype(BF16), bs_full, b)
    os_, lses = [], []
    for qkv, dilation in zip(qkvs, GROUP_DILATIONS):
        length = s // dilation
        o, lse = _attn_group(qkv.reshape(b * dilation, length, 3 * ATTN_WIDTH))
        os_.append(o.reshape(b, dilation, length, ATTN_WIDTH))
        lses.append(lse.reshape(b, dilation, length, LANES))
    out = _post(x2, os_, lses, sgu, gates,
                w_proj_attn[0].astype(BF16), w_proj_sgu[0].astype(BF16), w_out[0].astype(BF16),
                norm2_g[0][None, :], w_ffn_gate[0].astype(BF16), w_ffn_up[0].astype(BF16),
                w_ffn_down[0].astype(BF16), final_g[None, :], b)
    return out.reshape(b, s, d)
```

```python
import jax
import jax.numpy as jnp
import numpy as np
from jax import lax
from jax.experimental import pallas as pl
from jax.experimental.pallas import tpu as pltpu

D_MODEL = 1024
HEAD_DIM = 64
HEADS = 8
GROUP_DILATIONS = (1, 4, 16)
N_DIL = len(GROUP_DILATIONS)
SPAN = 128
ATTN_WIDTH = HEADS * HEAD_DIM
ROPE_DIM = HEAD_DIM // 4
ROPE_HALF = ROPE_DIM // 2
ROPE_THETA = 500000.0
SGU_CHUNK = 128
SGU_GROUPS = 8
SGU_WIDTH = D_MODEL // 2
SGU_GROUP_DIM = SGU_WIDTH // SGU_GROUPS
D_FF = 2816
QKV_COLS = 3 * N_DIL * ATTN_WIDTH
UV_OFF = QKV_COLS
GATE_OFF = QKV_COLS + 2 * SGU_WIDTH
IN_COLS = GATE_OFF + 2 * D_MODEL
EPS = 1e-6

LANES = 128
V7X_VMEM_BYTES = 64 * 1024 * 1024
VMEM_LIMIT = V7X_VMEM_BYTES - 8 * 1024 * 1024

IN_TM = 512
POST_TM = 512
ATTN_QB = 512
COL_CHUNK = 512

BF16 = jnp.bfloat16
F32 = jnp.float32


def _resident(shape):
    nd = len(shape)
    return pl.BlockSpec(shape, lambda *_: (0,) * nd, pipeline_mode=pl.Buffered(1))


def _dot(a, b):
    return jnp.dot(a, b, preferred_element_type=F32)


def _in_proj_kernel(x_ref, pos_ref, g1_ref, w_ref, freq_ref, lng_ref, lnb_ref, ws_ref, bs_ref,
                    qkv0_ref, qkv1_ref, qkv2_ref, sgu_ref, gates_ref, st_ref):
    tm = x_ref.shape[0]
    x = x_ref[...]
    h = (x * lax.rsqrt(jnp.mean(x * x, axis=-1, keepdims=True) + EPS) * g1_ref[...]).astype(BF16)

    ang = pos_ref[...].astype(F32) * freq_ref[...]
    cos = jnp.cos(ang)
    sin = jnp.sin(ang)
    lane = lax.broadcasted_iota(jnp.int32, (1, LANES), 1) % HEAD_DIM
    sin_up = jnp.where((lane >= ROPE_HALF) & (lane < ROPE_DIM), sin, 0.0)
    sin_dn = jnp.where(lane < ROPE_HALF, -sin, 0.0)

    qkv_refs = (qkv0_ref, qkv1_ref, qkv2_ref)
    for which in range(3):
        for g, dilation in enumerate(GROUP_DILATIONS):
            c = (which * N_DIL + g) * ATTN_WIDTH
            y = _dot(h, w_ref[:, c:c + ATTN_WIDTH])
            out_ref = qkv_refs[g]
            o0 = which * ATTN_WIDTH
            if which < 2:
                scale = HEAD_DIM ** -0.5 * np.log2(np.e) if which == 0 else 1.0
                for l in range(0, ATTN_WIDTH, LANES):
                    t = y[:, l:l + LANES]
                    t = (t * cos + pltpu.roll(t, ROPE_HALF, 1) * sin_up
                         + pltpu.roll(t, LANES - ROPE_HALF, 1) * sin_dn) * scale
                    if dilation == 1:
                        out_ref[0, :, o0 + l:o0 + l + LANES] = t.astype(BF16)
                    else:
                        st_ref[l // LANES] = t
            elif dilation == 1:
                out_ref[0, :, o0:o0 + ATTN_WIDTH] = y.astype(BF16)
            else:
                for l in range(0, ATTN_WIDTH, LANES):
                    st_ref[l // LANES] = y[:, l:l + LANES]
            if dilation > 1:
                for r in range(dilation):
                    for l in range(0, ATTN_WIDTH, LANES):
                        out_ref[r, :, o0 + l:o0 + l + LANES] = st_ref[
                            l // LANES, pl.ds(r, tm // dilation, stride=dilation), :].astype(BF16)

    def gelu(y):
        return 0.5 * y * (1.0 + lax.erf(y * np.float32(np.sqrt(0.5))))

    u = gelu(_dot(h, w_ref[:, UV_OFF:UV_OFF + SGU_WIDTH]))
    v = gelu(_dot(h, w_ref[:, UV_OFF + SGU_WIDTH:UV_OFF + 2 * SGU_WIDTH]))
    mu = jnp.mean(v, axis=-1, keepdims=True)
    vc = v - mu
    v = vc * lax.rsqrt(jnp.mean(vc * vc, axis=-1, keepdims=True) + EPS) * lng_ref[...] + lnb_ref[...]
    v = v.astype(BF16)

    row = lax.broadcasted_iota(jnp.int32, (SGU_CHUNK, SGU_GROUPS * SGU_CHUNK), 0)
    col = lax.broadcasted_iota(jnp.int32, (SGU_CHUNK, SGU_GROUPS * SGU_CHUNK), 1) % SGU_CHUNK
    w_causal = jnp.where(col <= row, ws_ref[...], jnp.zeros((), BF16))
    lane_group = lax.broadcasted_iota(jnp.int32, (1, SGU_WIDTH), 1) // SGU_GROUP_DIM
    for r0 in range(0, tm, SGU_CHUNK):
        vch = v[r0:r0 + SGU_CHUNK]
        v_bd = jnp.concatenate(
            [jnp.where(lane_group == g, vch, jnp.zeros((), BF16)) for g in range(SGU_GROUPS)], axis=0)
        mixed = _dot(w_causal, v_bd) + bs_ref[...]
        sgu_ref[r0:r0 + SGU_CHUNK, :] = (u[r0:r0 + SGU_CHUNK] * mixed).astype(BF16)

    for c in range(GATE_OFF, IN_COLS, COL_CHUNK):
        y = _dot(h, w_ref[:, c:c + COL_CHUNK])
        gates_ref[:, c - GATE_OFF:c - GATE_OFF + COL_CHUNK] = jax.nn.sigmoid(y).astype(BF16)


def _in_proj(x2, pos2, g1, w_in, freq, lng, lnb, ws_all, bs_full, batch):
    n = x2.shape[0]
    seq = n // batch
    tiles = seq // IN_TM
    row = lambda width: pl.BlockSpec((IN_TM, width), lambda i: (i, 0))

    def strided(d):
        return pl.BlockSpec((None, d, IN_TM // d, 3 * ATTN_WIDTH),
                            lambda i: (i // tiles, 0, i % tiles, 0))

    return pl.pallas_call(
        _in_proj_kernel,
        grid=(n // IN_TM,),
        in_specs=[
            row(D_MODEL), row(1),
            _resident((1, D_MODEL)), _resident((D_MODEL, IN_COLS)), _resident((1, LANES)),
            _resident((1, SGU_WIDTH)), _resident((1, SGU_WIDTH)),
            _resident((SGU_CHUNK, SGU_GROUPS * SGU_CHUNK)), _resident((SGU_CHUNK, SGU_WIDTH)),
        ],
        out_specs=[strided(d) for d in GROUP_DILATIONS] + [row(SGU_WIDTH), row(2 * D_MODEL)],
        out_shape=[jax.ShapeDtypeStruct((batch, d, seq // d, 3 * ATTN_WIDTH), BF16)
                   for d in GROUP_DILATIONS] + [
            jax.ShapeDtypeStruct((n, SGU_WIDTH), BF16),
            jax.ShapeDtypeStruct((n, 2 * D_MODEL), BF16),
        ],
        scratch_shapes=[pltpu.VMEM((ATTN_WIDTH // LANES, IN_TM, LANES), F32)],
        compiler_params=pltpu.CompilerParams(
            dimension_semantics=("arbitrary",), vmem_limit_bytes=VMEM_LIMIT),
        name="in_proj",
    )(x2, pos2, g1, w_in, freq, lng, lnb, ws_all, bs_full)


def _attn_kernel(q_ref, k_ref, kp_ref, v_ref, vp_ref, o_ref, stats_ref):
    first = pl.program_id(1) == 0
    qi = lax.broadcasted_iota(jnp.int32, (SPAN, 2 * SPAN), 0)
    kj = lax.broadcasted_iota(jnp.int32, (SPAN, 2 * SPAN), 1)
    band = (kj >= qi) & (kj <= qi + SPAN)
    first_lo = jnp.where(first, SPAN, 0)
    lane = lax.broadcasted_iota(jnp.int32, (1, LANES), 1)
    lo_head = lane < HEAD_DIM
    zero = jnp.zeros((), BF16)

    for j in range(0, ATTN_QB, SPAN):
        if j == 0:
            mask = band & (kj >= first_lo)
        else:
            mask = band
        stats = jnp.zeros((SPAN, LANES), F32)
        for p in range(0, ATTN_WIDTH, LANES):
            q = q_ref[0, j:j + SPAN, p:p + LANES]
            if j == 0:
                k = jnp.concatenate([kp_ref[0, :, p:p + LANES], k_ref[0, 0:SPAN, p:p + LANES]], axis=0)
                v = jnp.concatenate([vp_ref[0, :, p:p + LANES], v_ref[0, 0:SPAN, p:p + LANES]], axis=0)
            else:
                k = k_ref[0, j - SPAN:j + SPAN, p:p + LANES]
                v = v_ref[0, j - SPAN:j + SPAN, p:p + LANES]
            outs = []
            for half, head_lanes in enumerate((lo_head, ~lo_head)):
                qh = jnp.where(head_lanes, q, zero)
                s = lax.dot_general(qh, k, (((1,), (1,)), ((), ())), preferred_element_type=F32)
                s = jnp.where(mask, s, -jnp.inf)
                m = jnp.max(s, axis=-1, keepdims=True)
                e = jnp.exp2(s - m)
                den = jnp.sum(e, axis=-1, keepdims=True)
                outs.append(_dot(e.astype(BF16), v))
                head = p // HEAD_DIM + half
                stats = jnp.where(lane == head, m, jnp.where(lane == HEADS + head, den, stats))
            o_ref[0, j:j + SPAN, p:p + LANES] = jnp.where(lo_head, outs[0], outs[1]).astype(BF16)
        stats_ref[0, j:j + SPAN, :] = stats


def _attn_group(qkv):
    rows, length, _ = qkv.shape
    assert length % ATTN_QB == 0
    nq = length // ATTN_QB
    sub = ATTN_QB // SPAN

    def cur(which):
        return pl.BlockSpec((1, ATTN_QB, ATTN_WIDTH), lambda r, i: (r, i, which))

    def prev(which):
        return pl.BlockSpec((1, SPAN, ATTN_WIDTH),
                            lambda r, i: (r, jnp.maximum(i * sub - 1, 0), which))

    out_spec = lambda width: pl.BlockSpec((1, ATTN_QB, width), lambda r, i: (r, i, 0))
    return pl.pallas_call(
        _attn_kernel,
        grid=(rows, nq),
        in_specs=[cur(0), cur(1), prev(1), cur(2), prev(2)],
        out_specs=[out_spec(ATTN_WIDTH), out_spec(LANES)],
        out_shape=[
            jax.ShapeDtypeStruct((rows, length, ATTN_WIDTH), BF16),
            jax.ShapeDtypeStruct((rows, length, LANES), F32),
        ],
        compiler_params=pltpu.CompilerParams(
            dimension_semantics=("arbitrary", "arbitrary"), vmem_limit_bytes=VMEM_LIMIT),
        name=f"attn_len{length}",
    )(qkv, qkv, qkv, qkv, qkv)


FF_CHUNKS = ((0, 1024), (1024, 2048), (2048, D_FF))


def _post_kernel(x_ref, o0_ref, o1_ref, o2_ref, l0_ref, l1_ref, l2_ref, sgu_ref, gates_ref,
                 wpa_ref, wps_ref, wo_ref, g2_ref, wg_ref, wu_ref, wd_ref, gf_ref, ex_ref,
                 out_ref, ff_ref, tok_l, tok_o):
    tm = x_ref.shape[0]

    def token_major(ref, scratch):
        dilation, _, width = ref.shape
        if dilation == 1:
            return ref[0].astype(F32)
        for r in range(dilation):
            for l in range(0, width, LANES):
                scratch[l // LANES, pl.ds(r, tm // dilation, stride=dilation), :] = (
                    ref[r, :, l:l + LANES].astype(F32))
        return jnp.concatenate([scratch[i] for i in range(width // LANES)], axis=1)

    def per_head_to_lanes(a):
        hi = a.astype(BF16)
        lo = (a - hi.astype(F32)).astype(BF16)
        return _dot(jnp.concatenate([hi, lo], axis=1), ex_ref[...])

    st = (token_major(l0_ref, None), token_major(l1_ref, tok_l.at[0]), token_major(l2_ref, tok_l.at[1]))
    is_head = lax.broadcasted_iota(jnp.int32, (1, LANES), 1) < HEADS
    mx = jnp.maximum(jnp.maximum(st[0], st[1]), st[2])
    t = [jnp.exp2(s - mx) for s in st]
    den = [pltpu.roll(s, LANES - HEADS, 1) for s in st]
    total = jnp.where(is_head, den[0] * t[0] + den[1] * t[1] + den[2] * t[2], 1.0)
    inv = 1.0 / total
    w = [jnp.where(is_head, tg * inv, 0.0) for tg in t]
    attn = (per_head_to_lanes(w[0]) * token_major(o0_ref, None)
            + per_head_to_lanes(w[1]) * token_major(o1_ref, tok_o.at[0])
            + per_head_to_lanes(w[2]) * token_major(o2_ref, tok_o.at[1])).astype(BF16)

    gate_a = gates_ref[:, 0:D_MODEL].astype(F32)
    gate_b = gates_ref[:, D_MODEL:2 * D_MODEL].astype(F32)
    merged = gate_a * _dot(attn, wpa_ref[...]) + gate_b * _dot(sgu_ref[...], wps_ref[...])
    x1 = x_ref[...] + _dot(merged.astype(BF16), wo_ref[...])

    h2 = (x1 * lax.rsqrt(jnp.mean(x1 * x1, axis=-1, keepdims=True) + EPS) * g2_ref[...]).astype(BF16)
    for c0, c1 in FF_CHUNKS:
        gate = _dot(h2, wg_ref[:, c0:c1])
        up = _dot(h2, wu_ref[:, c0:c1])
        ff_ref[:, c0:c1] = (gate * jax.nn.sigmoid(gate) * up).astype(BF16)
    x2 = x1 + _dot(ff_ref[...], wd_ref[...])
    out_ref[...] = x2 * lax.rsqrt(jnp.mean(x2 * x2, axis=-1, keepdims=True) + EPS) * gf_ref[...]


def _post(x2, os_, lses, sgu, gates, wpa, wps, wo, g2, wg, wu, wd, gf, batch):
    n = x2.shape[0]
    seq = n // batch
    tiles = seq // POST_TM
    row = lambda width: pl.BlockSpec((POST_TM, width), lambda i: (i, 0))

    def strided(d, width):
        return pl.BlockSpec((None, d, POST_TM // d, width),
                            lambda i: (i // tiles, 0, i % tiles, 0))

    head_of_lane = np.arange(ATTN_WIDTH) // HEAD_DIM
    expand = (np.arange(2 * LANES)[:, None] % LANES == head_of_lane[None, :])
    expand = jnp.asarray(expand, dtype=BF16)

    return pl.pallas_call(
        _post_kernel,
        grid=(n // POST_TM,),
        in_specs=[row(D_MODEL)] + [strided(d, ATTN_WIDTH) for d in GROUP_DILATIONS]
        + [strided(d, LANES) for d in GROUP_DILATIONS] + [
            row(SGU_WIDTH), row(2 * D_MODEL),
            _resident((ATTN_WIDTH, D_MODEL)), _resident((SGU_WIDTH, D_MODEL)),
            _resident((D_MODEL, D_MODEL)), _resident((1, D_MODEL)),
            _resident((D_MODEL, D_FF)), _resident((D_MODEL, D_FF)), _resident((D_FF, D_MODEL)),
            _resident((1, D_MODEL)), _resident((2 * LANES, ATTN_WIDTH)),
        ],
        out_specs=row(D_MODEL),
        out_shape=jax.ShapeDtypeStruct((n, D_MODEL), F32),
        scratch_shapes=[pltpu.VMEM((POST_TM, D_FF), BF16),
                        pltpu.VMEM((2, 1, POST_TM, LANES), F32),
                        pltpu.VMEM((2, ATTN_WIDTH // LANES, POST_TM, LANES), F32)],
        compiler_params=pltpu.CompilerParams(
            dimension_semantics=("arbitrary",), vmem_limit_bytes=VMEM_LIMIT),
        name="post",
    )(x2, *os_, *lses, sgu, gates, wpa, wps, wo, g2, wg, wu, wd, gf, expand)


def _rope_freq_row():
    inv_freq = ROPE_THETA ** (-jnp.arange(0, ROPE_DIM, 2, dtype=F32) / ROPE_DIM)
    per_head = jnp.concatenate([inv_freq, inv_freq, jnp.zeros((HEAD_DIM - ROPE_DIM,), F32)])
    return jnp.tile(per_head, LANES // HEAD_DIM)[None, :]


def kernel(x, positions, norm1_g, w_in, sgu_ln_g, sgu_ln_b, w_spatial, b_spatial, w_proj_attn,
           w_proj_sgu, w_out, norm2_g, w_ffn_gate, w_ffn_up, w_ffn_down, final_g):
    b, s, d = x.shape
    assert w_in.shape[0] == 1, "single-layer block"
    x2 = x.reshape(b * s, d)
    pos2 = positions.reshape(b * s, 1)
    ws_all = jnp.transpose(w_spatial[0], (1, 0, 2)).reshape(SGU_CHUNK, SGU_GROUPS * SGU_CHUNK)
    bs_full = jnp.repeat(jnp.transpose(b_spatial[0]), SGU_GROUP_DIM, axis=1)
    *qkvs, sgu, gates = _in_proj(
        x2, pos2, norm1_g[0][None, :], w_in[0].astype(BF16), _rope_freq_row(),
        sgu_ln_g[0][None, :], sgu_ln_b[0][None, :], ws_all.astype(BF16), bs_full, b)
    os_, lses = [], []
    for qkv, dilation in zip(qkvs, GROUP_DILATIONS):
        length = s // dilation
        o, lse = _attn_group(qkv.reshape(b * dilation, length, 3 * ATTN_WIDTH))
        os_.append(o.reshape(b, dilation, length, ATTN_WIDTH))
        lses.append(lse.reshape(b, dilation, length, LANES))
    out = _post(x2, os_, lses, sgu, gates,
                w_proj_attn[0].astype(BF16), w_proj_sgu[0].astype(BF16), w_out[0].astype(BF16),
                norm2_g[0][None, :], w_ffn_gate[0].astype(BF16), w_ffn_up[0].astype(BF16),
                w_ffn_down[0].astype(BF16), final_g[None, :], b)
    return out.reshape(b, s, d)
```

```python
import jax
import jax.numpy as jnp
import numpy as np
from jax import lax
from jax.experimental import pallas as pl
from jax.experimental.pallas import tpu as pltpu

D_MODEL = 1024
HEAD_DIM = 64
HEADS = 8
GROUP_DILATIONS = (1, 4, 16)
N_DIL = len(GROUP_DILATIONS)
SPAN = 128
ATTN_WIDTH = HEADS * HEAD_DIM
ROPE_DIM = HEAD_DIM // 4
ROPE_HALF = ROPE_DIM // 2
ROPE_THETA = 500000.0
SGU_CHUNK = 128
SGU_GROUPS = 8
SGU_WIDTH = D_MODEL // 2
SGU_GROUP_DIM = SGU_WIDTH // SGU_GROUPS
D_FF = 2816
QKV_COLS = 3 * N_DIL * ATTN_WIDTH
UV_OFF = QKV_COLS
GATE_OFF = QKV_COLS + 2 * SGU_WIDTH
IN_COLS = GATE_OFF + 2 * D_MODEL
EPS = 1e-6

LANES = 128
V7X_VMEM_BYTES = 64 * 1024 * 1024
VMEM_LIMIT = V7X_VMEM_BYTES - 8 * 1024 * 1024

IN_TM = 512
POST_TM = 512
ATTN_QB = 512
COL_CHUNK = 512

BF16 = jnp.bfloat16
F32 = jnp.float32


def _resident(shape):
    nd = len(shape)
    return pl.BlockSpec(shape, lambda *_: (0,) * nd, pipeline_mode=pl.Buffered(1))


def _dot(a, b):
    return jnp.dot(a, b, preferred_element_type=F32)


def _in_proj_kernel(x_ref, pos_ref, g1_ref, w_ref, freq_ref, lng_ref, lnb_ref, ws_ref, bs_ref,
                    qkv0_ref, qkv1_ref, qkv2_ref, sgu_ref, gates_ref, st_ref):
    tm = x_ref.shape[0]
    x = x_ref[...]
    h = (x * lax.rsqrt(jnp.mean(x * x, axis=-1, keepdims=True) + EPS) * g1_ref[...]).astype(BF16)

    ang = pos_ref[...].astype(F32) * freq_ref[...]
    cos = jnp.cos(ang)
    sin = jnp.sin(ang)
    lane = lax.broadcasted_iota(jnp.int32, (1, LANES), 1) % HEAD_DIM
    sin_up = jnp.where((lane >= ROPE_HALF) & (lane < ROPE_DIM), sin, 0.0)
    sin_dn = jnp.where(lane < ROPE_HALF, -sin, 0.0)

    qkv_refs = (qkv0_ref, qkv1_ref, qkv2_ref)
    for which in range(3):
        for g, dilation in enumerate(GROUP_DILATIONS):
            c = (which * N_DIL + g) * ATTN_WIDTH
            y = _dot(h, w_ref[:, c:c + ATTN_WIDTH])
            out_ref = qkv_refs[g]
            o0 = which * ATTN_WIDTH
            if which < 2:
                scale = HEAD_DIM ** -0.5 * np.log2(np.e) if which == 0 else 1.0
                for l in range(0, ATTN_WIDTH, LANES):
                    t = y[:, l:l + LANES]
                    t = (t * cos + pltpu.roll(t, ROPE_HALF, 1) * sin_up
                         + pltpu.roll(t, LANES - ROPE_HALF, 1) * sin_dn) * scale
                    if dilation == 1:
                        out_ref[0, :, o0 + l:o0 + l + LANES] = t.astype(BF16)
                    else:
                        st_ref[l // LANES] = t
            elif dilation == 1:
                out_ref[0, :, o0:o0 + ATTN_WIDTH] = y.astype(BF16)
            else:
                for l in range(0, ATTN_WIDTH, LANES):
                    st_ref[l // LANES] = y[:, l:l + LANES]
            if dilation > 1:
                for r in range(dilation):
                    for l in range(0, ATTN_WIDTH, LANES):
                        out_ref[r, :, o0 + l:o0 + l + LANES] = st_ref[
                            l // LANES, pl.ds(r, tm // dilation, stride=dilation), :].astype(BF16)

    def gelu(y):
        return 0.5 * y * (1.0 + lax.erf(y * np.float32(np.sqrt(0.5))))

    u = gelu(_dot(h, w_ref[:, UV_OFF:UV_OFF + SGU_WIDTH]))
    v = gelu(_dot(h, w_ref[:, UV_OFF + SGU_WIDTH:UV_OFF + 2 * SGU_WIDTH]))
    mu = jnp.mean(v, axis=-1, keepdims=True)
    vc = v - mu
    v = vc * lax.rsqrt(jnp.mean(vc * vc, axis=-1, keepdims=True) + EPS) * lng_ref[...] + lnb_ref[...]
    v = v.astype(BF16)

    row = lax.broadcasted_iota(jnp.int32, (SGU_CHUNK, SGU_GROUPS * SGU_CHUNK), 0)
    col = lax.broadcasted_iota(jnp.int32, (SGU_CHUNK, SGU_GROUPS * SGU_CHUNK), 1) % SGU_CHUNK
    w_causal = jnp.where(col <= row, ws_ref[...], jnp.zeros((), BF16))
    lane_group = lax.broadcasted_iota(jnp.int32, (1, SGU_WIDTH), 1) // SGU_GROUP_DIM
    for r0 in range(0, tm, SGU_CHUNK):
        vch = v[r0:r0 + SGU_CHUNK]
        v_bd = jnp.concatenate(
            [jnp.where(lane_group == g, vch, jnp.zeros((), BF16)) for g in range(SGU_GROUPS)], axis=0)
        mixed = _dot(w_causal, v_bd) + bs_ref[...]
        sgu_ref[r0:r0 + SGU_CHUNK, :] = (u[r0:r0 + SGU_CHUNK] * mixed).astype(BF16)

    for c in range(GATE_OFF, IN_COLS, COL_CHUNK):
        y = _dot(h, w_ref[:, c:c + COL_CHUNK])
        gates_ref[:, c - GATE_OFF:c - GATE_OFF + COL_CHUNK] = jax.nn.sigmoid(y).astype(BF16)


def _in_proj(x2, pos2, g1, w_in, freq, lng, lnb, ws_all, bs_full, batch):
    n = x2.shape[0]
    seq = n // batch
    tiles = seq // IN_TM
    row = lambda width: pl.BlockSpec((IN_TM, width), lambda i: (i, 0))

    def strided(d):
        return pl.BlockSpec((None, d, IN_TM // d, 3 * ATTN_WIDTH),
                            lambda i: (i // tiles, 0, i % tiles, 0))

    return pl.pallas_call(
        _in_proj_kernel,
        grid=(n // IN_TM,),
        in_specs=[
            row(D_MODEL), row(1),
            _resident((1, D_MODEL)), _resident((D_MODEL, IN_COLS)), _resident((1, LANES)),
            _resident((1, SGU_WIDTH)), _resident((1, SGU_WIDTH)),
            _resident((SGU_CHUNK, SGU_GROUPS * SGU_CHUNK)), _resident((SGU_CHUNK, SGU_WIDTH)),
        ],
        out_specs=[strided(d) for d in GROUP_DILATIONS] + [row(SGU_WIDTH), row(2 * D_MODEL)],
        out_shape=[jax.ShapeDtypeStruct((batch, d, seq // d, 3 * ATTN_WIDTH), BF16)
                   for d in GROUP_DILATIONS] + [
            jax.ShapeDtypeStruct((n, SGU_WIDTH), BF16),
            jax.ShapeDtypeStruct((n, 2 * D_MODEL), BF16),
        ],
        scratch_shapes=[pltpu.VMEM((ATTN_WIDTH // LANES, IN_TM, LANES), F32)],
        compiler_params=pltpu.CompilerParams(
            dimension_semantics=("arbitrary",), vmem_limit_bytes=VMEM_LIMIT),
        name="in_proj",
    )(x2, pos2, g1, w_in, freq, lng, lnb, ws_all, bs_full)


def _attn_kernel(q_ref, k_ref, kp_ref, v_ref, vp_ref, o_ref, stats_ref, bias_ref):
    first = pl.program_id(1) == 0
    qi = lax.broadcasted_iota(jnp.int32, (SPAN, 2 * SPAN), 0)
    kj = lax.broadcasted_iota(jnp.int32, (SPAN, 2 * SPAN), 1)
    band = (kj >= qi) & (kj <= qi + SPAN)
    first_lo = jnp.where(first, SPAN, 0)
    lane = lax.broadcasted_iota(jnp.int32, (1, LANES), 1)
    lo_head = lane < HEAD_DIM
    zero = jnp.zeros((), BF16)
    bias_ref[0] = jnp.where(band & (kj >= first_lo), 0.0, -jnp.inf)
    bias_ref[1] = jnp.where(band, 0.0, -jnp.inf)

    for j in range(0, ATTN_QB, SPAN):
        bias = bias_ref.at[0 if j == 0 else 1]
        stats_ref[0, j:j + SPAN, :] = jnp.zeros((SPAN, LANES), F32)
        for p in range(0, ATTN_WIDTH, LANES):
            q = q_ref[0, j:j + SPAN, p:p + LANES]
            if j == 0:
                k = jnp.concatenate([kp_ref[0, :, p:p + LANES], k_ref[0, 0:SPAN, p:p + LANES]], axis=0)
                v = jnp.concatenate([vp_ref[0, :, p:p + LANES], v_ref[0, 0:SPAN, p:p + LANES]], axis=0)
            else:
                k = k_ref[0, j - SPAN:j + SPAN, p:p + LANES]
                v = v_ref[0, j - SPAN:j + SPAN, p:p + LANES]
            qq = jnp.concatenate([jnp.where(lo_head, q, zero), jnp.where(lo_head, zero, q)], axis=0)
            s = lax.dot_general(qq, k, (((1,), (1,)), ((), ())), preferred_element_type=F32)
            s = s + jnp.concatenate([bias[...], bias[...]], axis=0)
            m = jnp.max(s, axis=-1, keepdims=True)
            e = jnp.exp2(s - m)
            den = jnp.sum(e, axis=-1, keepdims=True)
            res = _dot(e.astype(BF16), v)
            for half in range(2):
                head = p // HEAD_DIM + half
                stats_ref[0, j:j + SPAN, head:head + 1] = m[half * SPAN:(half + 1) * SPAN]
                stats_ref[0, j:j + SPAN, HEADS + head:HEADS + head + 1] = den[half * SPAN:(half + 1) * SPAN]
            o_ref[0, j:j + SPAN, p:p + LANES] = jnp.where(lo_head, res[:SPAN], res[SPAN:]).astype(BF16)


def _attn_group(qkv):
    rows, length, _ = qkv.shape
    assert length % ATTN_QB == 0
    nq = length // ATTN_QB
    sub = ATTN_QB // SPAN

    def cur(which):
        return pl.BlockSpec((1, ATTN_QB, ATTN_WIDTH), lambda r, i: (r, i, which))

    def prev(which):
        return pl.BlockSpec((1, SPAN, ATTN_WIDTH),
                            lambda r, i: (r, jnp.maximum(i * sub - 1, 0), which))

    out_spec = lambda width: pl.BlockSpec((1, ATTN_QB, width), lambda r, i: (r, i, 0))
    return pl.pallas_call(
        _attn_kernel,
        grid=(rows, nq),
        in_specs=[cur(0), cur(1), prev(1), cur(2), prev(2)],
        out_specs=[out_spec(ATTN_WIDTH), out_spec(LANES)],
        out_shape=[
            jax.ShapeDtypeStruct((rows, length, ATTN_WIDTH), BF16),
            jax.ShapeDtypeStruct((rows, length, LANES), F32),
        ],
        scratch_shapes=[pltpu.VMEM((2, SPAN, 2 * SPAN), F32)],
        compiler_params=pltpu.CompilerParams(
            dimension_semantics=("arbitrary", "arbitrary"), vmem_limit_bytes=VMEM_LIMIT),
        name=f"attn_len{length}",
    )(qkv, qkv, qkv, qkv, qkv)


FF_CHUNKS = ((0, 1024), (1024, 2048), (2048, D_FF))


def _post_kernel(x_ref, o0_ref, o1_ref, o2_ref, l0_ref, l1_ref, l2_ref, sgu_ref, gates_ref,
                 wpa_ref, wps_ref, wo_ref, g2_ref, wg_ref, wu_ref, wd_ref, gf_ref, ex_ref,
                 out_ref, ff_ref, tok_l, tok_o):
    tm = x_ref.shape[0]

    def token_major(ref, scratch):
        dilation, _, width = ref.shape
        if dilation == 1:
            return ref[0].astype(F32)
        for r in range(dilation):
            for l in range(0, width, LANES):
                scratch[l // LANES, pl.ds(r, tm // dilation, stride=dilation), :] = (
                    ref[r, :, l:l + LANES].astype(F32))
        return jnp.concatenate([scratch[i] for i in range(width // LANES)], axis=1)

    def per_head_to_lanes(a):
        hi = a.astype(BF16)
        lo = (a - hi.astype(F32)).astype(BF16)
        return _dot(jnp.concatenate([hi, lo], axis=1), ex_ref[...])

    st = (token_major(l0_ref, None), token_major(l1_ref, tok_l.at[0]), token_major(l2_ref, tok_l.at[1]))
    is_head = lax.broadcasted_iota(jnp.int32, (1, LANES), 1) < HEADS
    mx = jnp.maximum(jnp.maximum(st[0], st[1]), st[2])
    t = [jnp.exp2(s - mx) for s in st]
    den = [pltpu.roll(s, LANES - HEADS, 1) for s in st]
    total = jnp.where(is_head, den[0] * t[0] + den[1] * t[1] + den[2] * t[2], 1.0)
    inv = 1.0 / total
    w = [jnp.where(is_head, tg * inv, 0.0) for tg in t]
    attn = (per_head_to_lanes(w[0]) * token_major(o0_ref, None)
            + per_head_to_lanes(w[1]) * token_major(o1_ref, tok_o.at[0])
            + per_head_to_lanes(w[2]) * token_major(o2_ref, tok_o.at[1])).astype(BF16)

    gate_a = gates_ref[:, 0:D_MODEL].astype(F32)
    gate_b = gates_ref[:, D_MODEL:2 * D_MODEL].astype(F32)
    merged = gate_a * _dot(attn, wpa_ref[...]) + gate_b * _dot(sgu_ref[...], wps_ref[...])
    x1 = x_ref[...] + _dot(merged.astype(BF16), wo_ref[...])

    h2 = (x1 * lax.rsqrt(jnp.mean(x1 * x1, axis=-1, keepdims=True) + EPS) * g2_ref[...]).astype(BF16)
    for c0, c1 in FF_CHUNKS:
        gate = _dot(h2, wg_ref[:, c0:c1])
        up = _dot(h2, wu_ref[:, c0:c1])
        ff_ref[:, c0:c1] = (gate * jax.nn.sigmoid(gate) * up).astype(BF16)
    x2 = x1 + _dot(ff_ref[...], wd_ref[...])
    out_ref[...] = x2 * lax.rsqrt(jnp.mean(x2 * x2, axis=-1, keepdims=True) + EPS) * gf_ref[...]


def _post(x2, os_, lses, sgu, gates, wpa, wps, wo, g2, wg, wu, wd, gf, batch):
    n = x2.shape[0]
    seq = n // batch
    tiles = seq // POST_TM
    row = lambda width: pl.BlockSpec((POST_TM, width), lambda i: (i, 0))

    def strided(d, width):
        return pl.BlockSpec((None, d, POST_TM // d, width),
                            lambda i: (i // tiles, 0, i % tiles, 0))

    head_of_lane = np.arange(ATTN_WIDTH) // HEAD_DIM
    expand = (np.arange(2 * LANES)[:, None] % LANES == head_of_lane[None, :])
    expand = jnp.asarray(expand, dtype=BF16)

    return pl.pallas_call(
        _post_kernel,
        grid=(n // POST_TM,),
        in_specs=[row(D_MODEL)] + [strided(d, ATTN_WIDTH) for d in GROUP_DILATIONS]
        + [strided(d, LANES) for d in GROUP_DILATIONS] + [
            row(SGU_WIDTH), row(2 * D_MODEL),
            _resident((ATTN_WIDTH, D_MODEL)), _resident((SGU_WIDTH, D_MODEL)),
            _resident((D_MODEL, D_MODEL)), _resident((1, D_MODEL)),
            _resident((D_MODEL, D_FF)), _resident((D_MODEL, D_FF)), _resident((D_FF, D_MODEL)),
            _resident((1, D_MODEL)), _resident((2 * LANES, ATTN_WIDTH)),
        ],
        out_specs=row(D_MODEL),
        out_shape=jax.ShapeDtypeStruct((n, D_MODEL), F32),
        scratch_shapes=[pltpu.VMEM((POST_TM, D_FF), BF16),
                        pltpu.VMEM((2, 1, POST_TM, LANES), F32),
                        pltpu.VMEM((2, ATTN_WIDTH // LANES, POST_TM, LANES), F32)],
        compiler_params=pltpu.CompilerParams(
            dimension_semantics=("arbitrary",), vmem_limit_bytes=VMEM_LIMIT),
        name="post",
    )(x2, *os_, *lses, sgu, gates, wpa, wps, wo, g2, wg, wu, wd, gf, expand)


def _rope_freq_row():
    inv_freq = ROPE_THETA ** (-jnp.arange(0, ROPE_DIM, 2, dtype=F32) / ROPE_DIM)
    per_head = jnp.concatenate([inv_freq, inv_freq, jnp.zeros((HEAD_DIM - ROPE_DIM,), F32)])
    return jnp.tile(per_head, LANES // HEAD_DIM)[None, :]


def kernel(x, positions, norm1_g, w_in, sgu_ln_g, sgu_ln_b, w_spatial, b_spatial, w_proj_attn,
           w_proj_sgu, w_out, norm2_g, w_ffn_gate, w_ffn_up, w_ffn_down, final_g):
    b, s, d = x.shape
    assert w_in.shape[0] == 1, "single-layer block"
    x2 = x.reshape(b * s, d)
    pos2 = positions.reshape(b * s, 1)
    ws_all = jnp.transpose(w_spatial[0], (1, 0, 2)).reshape(SGU_CHUNK, SGU_GROUPS * SGU_CHUNK)
    bs_full = jnp.repeat(jnp.transpose(b_spatial[0]), SGU_GROUP_DIM, axis=1)
    *qkvs, sgu, gates = _in_proj(
        x2, pos2, norm1_g[0][None, :], w_in[0].astype(BF16), _rope_freq_row(),
        sgu_ln_g[0][None, :], sgu_ln_b[0][None, :], ws_all.astype(BF16), bs_full, b)
    os_, lses = [], []
    for qkv, dilation in zip(qkvs, GROUP_DILATIONS):
        length = s // dilation
        o, lse = _attn_group(qkv.reshape(b * dilation, length, 3 * ATTN_WIDTH))
        os_.append(o.reshape(b, dilation, length, ATTN_WIDTH))
        lses.append(lse.reshape(b, dilation, length, LANES))
    out = _post(x2, os_, lses, sgu, gates,
                w_proj_attn[0].astype(BF16), w_proj_sgu[0].astype(BF16), w_out[0].astype(BF16),
                norm2_g[0][None, :], w_ffn_gate[0].astype(BF16), w_ffn_up[0].astype(BF16),
                w_ffn_down[0].astype(BF16), final_g[None, :], b)
    return out.reshape(b, s, d)
```

```python
import jax
import jax.numpy as jnp
import numpy as np
from jax import lax
from jax.experimental import pallas as pl
from jax.experimental.pallas import tpu as pltpu

D_MODEL = 1024
HEAD_DIM = 64
HEADS = 8
GROUP_DILATIONS = (1, 4, 16)
N_DIL = len(GROUP_DILATIONS)
SPAN = 128
ATTN_WIDTH = HEADS * HEAD_DIM
ROPE_DIM = HEAD_DIM // 4
ROPE_HALF = ROPE_DIM // 2
ROPE_THETA = 500000.0
SGU_CHUNK = 128
SGU_GROUPS = 8
SGU_WIDTH = D_MODEL // 2
SGU_GROUP_DIM = SGU_WIDTH // SGU_GROUPS
D_FF = 2816
QKV_COLS = 3 * N_DIL * ATTN_WIDTH
UV_OFF = QKV_COLS
GATE_OFF = QKV_COLS + 2 * SGU_WIDTH
IN_COLS = GATE_OFF + 2 * D_MODEL
EPS = 1e-6

LANES = 128
V7X_VMEM_BYTES = 64 * 1024 * 1024
VMEM_LIMIT = V7X_VMEM_BYTES - 8 * 1024 * 1024

IN_TM = 512
POST_TM = 512
ATTN_QB = 512
COL_CHUNK = 512

BF16 = jnp.bfloat16
F32 = jnp.float32


def _resident(shape):
    nd = len(shape)
    return pl.BlockSpec(shape, lambda *_: (0,) * nd, pipeline_mode=pl.Buffered(1))


def _dot(a, b):
    return jnp.dot(a, b, preferred_element_type=F32)


def _in_proj_kernel(x_ref, pos_ref, g1_ref, w_ref, freq_ref, lng_ref, lnb_ref, ws_ref, bs_ref,
                    qkv0_ref, qkv1_ref, qkv2_ref, sgu_ref, gates_ref, st_ref):
    tm = x_ref.shape[0]
    x = x_ref[...]
    h = (x * lax.rsqrt(jnp.mean(x * x, axis=-1, keepdims=True) + EPS) * g1_ref[...]).astype(BF16)

    ang = pos_ref[...].astype(F32) * freq_ref[...]
    cos = jnp.cos(ang)
    lane = lax.broadcasted_iota(jnp.int32, (1, LANES), 1)
    sin = jnp.where(lane < LANES // 2, -jnp.sin(ang), jnp.sin(ang))

    qkv_refs = (qkv0_ref, qkv1_ref, qkv2_ref)
    for which in range(3):
        for g, dilation in enumerate(GROUP_DILATIONS):
            c = (which * N_DIL + g) * ATTN_WIDTH
            y = _dot(h, w_ref[:, c:c + ATTN_WIDTH])
            out_ref = qkv_refs[g]
            o0 = which * ATTN_WIDTH
            if which < 2:
                scale = HEAD_DIM ** -0.5 * np.log2(np.e) if which == 0 else 1.0
                for l in range(0, ATTN_WIDTH, LANES):
                    t = y[:, l:l + LANES]
                    t = (t * cos + pltpu.roll(t, LANES // 2, 1) * sin) * scale
                    if dilation == 1:
                        out_ref[0, :, o0 + l:o0 + l + LANES] = t.astype(BF16)
                    else:
                        st_ref[l // LANES] = t
            elif dilation == 1:
                out_ref[0, :, o0:o0 + ATTN_WIDTH] = y.astype(BF16)
            else:
                for l in range(0, ATTN_WIDTH, LANES):
                    st_ref[l // LANES] = y[:, l:l + LANES]
            if dilation > 1:
                for r in range(dilation):
                    for l in range(0, ATTN_WIDTH, LANES):
                        out_ref[r, :, o0 + l:o0 + l + LANES] = st_ref[
                            l // LANES, pl.ds(r, tm // dilation, stride=dilation), :].astype(BF16)

    def gelu(y):
        return 0.5 * y * (1.0 + lax.erf(y * np.float32(np.sqrt(0.5))))

    u = gelu(_dot(h, w_ref[:, UV_OFF:UV_OFF + SGU_WIDTH]))
    v = gelu(_dot(h, w_ref[:, UV_OFF + SGU_WIDTH:UV_OFF + 2 * SGU_WIDTH]))
    mu = jnp.mean(v, axis=-1, keepdims=True)
    vc = v - mu
    v = vc * lax.rsqrt(jnp.mean(vc * vc, axis=-1, keepdims=True) + EPS) * lng_ref[...] + lnb_ref[...]
    v = v.astype(BF16)

    row = lax.broadcasted_iota(jnp.int32, (SGU_CHUNK, SGU_GROUPS * SGU_CHUNK), 0)
    col = lax.broadcasted_iota(jnp.int32, (SGU_CHUNK, SGU_GROUPS * SGU_CHUNK), 1) % SGU_CHUNK
    w_causal = jnp.where(col <= row, ws_ref[...], jnp.zeros((), BF16))
    lane_group = lax.broadcasted_iota(jnp.int32, (1, SGU_WIDTH), 1) // SGU_GROUP_DIM
    for r0 in range(0, tm, SGU_CHUNK):
        vch = v[r0:r0 + SGU_CHUNK]
        v_bd = jnp.concatenate(
            [jnp.where(lane_group == g, vch, jnp.zeros((), BF16)) for g in range(SGU_GROUPS)], axis=0)
        mixed = _dot(w_causal, v_bd) + bs_ref[...]
        sgu_ref[r0:r0 + SGU_CHUNK, :] = (u[r0:r0 + SGU_CHUNK] * mixed).astype(BF16)

    for c in range(GATE_OFF, IN_COLS, COL_CHUNK):
        y = _dot(h, w_ref[:, c:c + COL_CHUNK])
        gates_ref[:, c - GATE_OFF:c - GATE_OFF + COL_CHUNK] = jax.nn.sigmoid(y).astype(BF16)


def _in_proj(x2, pos2, g1, w_in, freq, lng, lnb, ws_all, bs_full, batch):
    n = x2.shape[0]
    seq = n // batch
    tiles = seq // IN_TM
    row = lambda width: pl.BlockSpec((IN_TM, width), lambda i: (i, 0))

    def strided(d):
        return pl.BlockSpec((None, d, IN_TM // d, 3 * ATTN_WIDTH),
                            lambda i: (i // tiles, 0, i % tiles, 0))

    return pl.pallas_call(
        _in_proj_kernel,
        grid=(n // IN_TM,),
        in_specs=[
            row(D_MODEL), row(1),
            _resident((1, D_MODEL)), _resident((D_MODEL, IN_COLS)), _resident((1, LANES)),
            _resident((1, SGU_WIDTH)), _resident((1, SGU_WIDTH)),
            _resident((SGU_CHUNK, SGU_GROUPS * SGU_CHUNK)), _resident((SGU_CHUNK, SGU_WIDTH)),
        ],
        out_specs=[strided(d) for d in GROUP_DILATIONS] + [row(SGU_WIDTH), row(2 * D_MODEL)],
        out_shape=[jax.ShapeDtypeStruct((batch, d, seq // d, 3 * ATTN_WIDTH), BF16)
                   for d in GROUP_DILATIONS] + [
            jax.ShapeDtypeStruct((n, SGU_WIDTH), BF16),
            jax.ShapeDtypeStruct((n, 2 * D_MODEL), BF16),
        ],
        scratch_shapes=[pltpu.VMEM((ATTN_WIDTH // LANES, IN_TM, LANES), F32)],
        compiler_params=pltpu.CompilerParams(
            dimension_semantics=("arbitrary",), vmem_limit_bytes=VMEM_LIMIT),
        name="in_proj",
    )(x2, pos2, g1, w_in, freq, lng, lnb, ws_all, bs_full)


def _attn_kernel(q_ref, k_ref, kp_ref, v_ref, vp_ref, o_ref, stats_ref, bias_ref):
    first = pl.program_id(1) == 0
    qi = lax.broadcasted_iota(jnp.int32, (SPAN, 2 * SPAN), 0)
    kj = lax.broadcasted_iota(jnp.int32, (SPAN, 2 * SPAN), 1)
    band = (kj >= qi) & (kj <= qi + SPAN)
    first_lo = jnp.where(first, SPAN, 0)
    lane = lax.broadcasted_iota(jnp.int32, (1, LANES), 1)
    lo_head = lane < HEAD_DIM
    q_first = _first_head_lanes(lane)
    zero = jnp.zeros((), BF16)
    bias_ref[0] = jnp.where(band & (kj >= first_lo), 0.0, -jnp.inf)
    bias_ref[1] = jnp.where(band, 0.0, -jnp.inf)

    for j in range(0, ATTN_QB, SPAN):
        bias = bias_ref.at[0 if j == 0 else 1]
        stats_ref[0, j:j + SPAN, :] = jnp.zeros((SPAN, LANES), F32)
        for p in range(0, ATTN_WIDTH, LANES):
            q = q_ref[0, j:j + SPAN, p:p + LANES]
            if j == 0:
                k = jnp.concatenate([kp_ref[0, :, p:p + LANES], k_ref[0, 0:SPAN, p:p + LANES]], axis=0)
                v = jnp.concatenate([vp_ref[0, :, p:p + LANES], v_ref[0, 0:SPAN, p:p + LANES]], axis=0)
            else:
                k = k_ref[0, j - SPAN:j + SPAN, p:p + LANES]
                v = v_ref[0, j - SPAN:j + SPAN, p:p + LANES]
            qq = jnp.concatenate([jnp.where(q_first, q, zero), jnp.where(q_first, zero, q)], axis=0)
            s = lax.dot_general(qq, k, (((1,), (1,)), ((), ())), preferred_element_type=F32)
            s = s + jnp.concatenate([bias[...], bias[...]], axis=0)
            m = jnp.max(s, axis=-1, keepdims=True)
            e = jnp.exp2(s - m)
            den = jnp.sum(e, axis=-1, keepdims=True)
            res = _dot(e.astype(BF16), v)
            for half in range(2):
                head = p // HEAD_DIM + half
                stats_ref[0, j:j + SPAN, head:head + 1] = m[half * SPAN:(half + 1) * SPAN]
                stats_ref[0, j:j + SPAN, HEADS + head:HEADS + head + 1] = den[half * SPAN:(half + 1) * SPAN]
            o_ref[0, j:j + SPAN, p:p + LANES] = jnp.where(lo_head, res[:SPAN], res[SPAN:]).astype(BF16)


def _attn_group(qkv):
    rows, length, _ = qkv.shape
    assert length % ATTN_QB == 0
    nq = length // ATTN_QB
    sub = ATTN_QB // SPAN

    def cur(which):
        return pl.BlockSpec((1, ATTN_QB, ATTN_WIDTH), lambda r, i: (r, i, which))

    def prev(which):
        return pl.BlockSpec((1, SPAN, ATTN_WIDTH),
                            lambda r, i: (r, jnp.maximum(i * sub - 1, 0), which))

    out_spec = lambda width: pl.BlockSpec((1, ATTN_QB, width), lambda r, i: (r, i, 0))
    return pl.pallas_call(
        _attn_kernel,
        grid=(rows, nq),
        in_specs=[cur(0), cur(1), prev(1), cur(2), prev(2)],
        out_specs=[out_spec(ATTN_WIDTH), out_spec(LANES)],
        out_shape=[
            jax.ShapeDtypeStruct((rows, length, ATTN_WIDTH), BF16),
            jax.ShapeDtypeStruct((rows, length, LANES), F32),
        ],
        scratch_shapes=[pltpu.VMEM((2, SPAN, 2 * SPAN), F32)],
        compiler_params=pltpu.CompilerParams(
            dimension_semantics=("arbitrary", "arbitrary"), vmem_limit_bytes=VMEM_LIMIT),
        name=f"attn_len{length}",
    )(qkv, qkv, qkv, qkv, qkv)


FF_CHUNKS = ((0, 1024), (1024, 2048), (2048, D_FF))


def _post_kernel(x_ref, o0_ref, o1_ref, o2_ref, l0_ref, l1_ref, l2_ref, sgu_ref, gates_ref,
                 wpa_ref, wps_ref, wo_ref, g2_ref, wg_ref, wu_ref, wd_ref, gf_ref, ex_ref,
                 out_ref, ff_ref, tok_l, tok_o):
    tm = x_ref.shape[0]

    def token_major(ref, scratch):
        dilation, _, width = ref.shape
        if dilation == 1:
            return ref[0].astype(F32)
        for r in range(dilation):
            for l in range(0, width, LANES):
                scratch[l // LANES, pl.ds(r, tm // dilation, stride=dilation), :] = (
                    ref[r, :, l:l + LANES].astype(F32))
        return jnp.concatenate([scratch[i] for i in range(width // LANES)], axis=1)

    def per_head_to_lanes(a):
        hi = a.astype(BF16)
        lo = (a - hi.astype(F32)).astype(BF16)
        return _dot(jnp.concatenate([hi, lo], axis=1), ex_ref[...])

    st = (token_major(l0_ref, None), token_major(l1_ref, tok_l.at[0]), token_major(l2_ref, tok_l.at[1]))
    is_head = lax.broadcasted_iota(jnp.int32, (1, LANES), 1) < HEADS
    mx = jnp.maximum(jnp.maximum(st[0], st[1]), st[2])
    t = [jnp.exp2(s - mx) for s in st]
    den = [pltpu.roll(s, LANES - HEADS, 1) for s in st]
    total = jnp.where(is_head, den[0] * t[0] + den[1] * t[1] + den[2] * t[2], 1.0)
    inv = 1.0 / total
    w = [jnp.where(is_head, tg * inv, 0.0) for tg in t]
    attn = (per_head_to_lanes(w[0]) * token_major(o0_ref, None)
            + per_head_to_lanes(w[1]) * token_major(o1_ref, tok_o.at[0])
            + per_head_to_lanes(w[2]) * token_major(o2_ref, tok_o.at[1])).astype(BF16)

    gate_a = gates_ref[:, 0:D_MODEL].astype(F32)
    gate_b = gates_ref[:, D_MODEL:2 * D_MODEL].astype(F32)
    merged = gate_a * _dot(attn, wpa_ref[...]) + gate_b * _dot(sgu_ref[...], wps_ref[...])
    x1 = x_ref[...] + _dot(merged.astype(BF16), wo_ref[...])

    h2 = (x1 * lax.rsqrt(jnp.mean(x1 * x1, axis=-1, keepdims=True) + EPS) * g2_ref[...]).astype(BF16)
    for c0, c1 in FF_CHUNKS:
        gate = _dot(h2, wg_ref[:, c0:c1])
        up = _dot(h2, wu_ref[:, c0:c1])
        ff_ref[:, c0:c1] = (gate * jax.nn.sigmoid(gate) * up).astype(BF16)
    x2 = x1 + _dot(ff_ref[...], wd_ref[...])
    out_ref[...] = x2 * lax.rsqrt(jnp.mean(x2 * x2, axis=-1, keepdims=True) + EPS) * gf_ref[...]


def _post(x2, os_, lses, sgu, gates, wpa, wps, wo, g2, wg, wu, wd, gf, batch):
    n = x2.shape[0]
    seq = n // batch
    tiles = seq // POST_TM
    row = lambda width: pl.BlockSpec((POST_TM, width), lambda i: (i, 0))

    def strided(d, width):
        return pl.BlockSpec((None, d, POST_TM // d, width),
                            lambda i: (i // tiles, 0, i % tiles, 0))

    head_of_lane = np.arange(ATTN_WIDTH) // HEAD_DIM
    expand = (np.arange(2 * LANES)[:, None] % LANES == head_of_lane[None, :])
    expand = jnp.asarray(expand, dtype=BF16)

    return pl.pallas_call(
        _post_kernel,
        grid=(n // POST_TM,),
        in_specs=[row(D_MODEL)] + [strided(d, ATTN_WIDTH) for d in GROUP_DILATIONS]
        + [strided(d, LANES) for d in GROUP_DILATIONS] + [
            row(SGU_WIDTH), row(2 * D_MODEL),
            _resident((ATTN_WIDTH, D_MODEL)), _resident((SGU_WIDTH, D_MODEL)),
            _resident((D_MODEL, D_MODEL)), _resident((1, D_MODEL)),
            _resident((D_MODEL, D_FF)), _resident((D_MODEL, D_FF)), _resident((D_FF, D_MODEL)),
            _resident((1, D_MODEL)), _resident((2 * LANES, ATTN_WIDTH)),
        ],
        out_specs=row(D_MODEL),
        out_shape=jax.ShapeDtypeStruct((n, D_MODEL), F32),
        scratch_shapes=[pltpu.VMEM((POST_TM, D_FF), BF16),
                        pltpu.VMEM((2, 1, POST_TM, LANES), F32),
                        pltpu.VMEM((2, ATTN_WIDTH // LANES, POST_TM, LANES), F32)],
        compiler_params=pltpu.CompilerParams(
            dimension_semantics=("arbitrary",), vmem_limit_bytes=VMEM_LIMIT),
        name="post",
    )(x2, *os_, *lses, sgu, gates, wpa, wps, wo, g2, wg, wu, wd, gf, expand)


def _rope_freq_row():
    inv_freq = ROPE_THETA ** (-jnp.arange(0, ROPE_DIM, 2, dtype=F32) / ROPE_DIM)
    half_slab = jnp.concatenate([inv_freq, inv_freq, jnp.zeros((LANES // 2 - ROPE_DIM,), F32)])
    return jnp.tile(half_slab, 2)[None, :]


def _pair_slab_order(w_in):
    qk_cols = 2 * N_DIL * ATTN_WIDTH
    rest = (HEAD_DIM - ROPE_DIM) // 2
    w = w_in[:, :qk_cols].reshape(D_MODEL, qk_cols // LANES, LANES // HEAD_DIM, HEAD_DIM)
    flat = lambda piece: piece.reshape(D_MODEL, qk_cols // LANES, -1)
    slabs = jnp.concatenate([
        flat(w[..., :ROPE_HALF]), flat(w[..., ROPE_DIM:ROPE_DIM + rest]),
        flat(w[..., ROPE_HALF:ROPE_DIM]), flat(w[..., ROPE_DIM + rest:])], axis=-1)
    return jnp.concatenate([slabs.reshape(D_MODEL, qk_cols), w_in[:, qk_cols:]], axis=1)


def _first_head_lanes(lane):
    m = lane % (LANES // 2)
    rest = (HEAD_DIM - ROPE_DIM) // 2
    return (m < ROPE_HALF) | ((m >= ROPE_DIM) & (m < ROPE_DIM + rest))


def kernel(x, positions, norm1_g, w_in, sgu_ln_g, sgu_ln_b, w_spatial, b_spatial, w_proj_attn,
           w_proj_sgu, w_out, norm2_g, w_ffn_gate, w_ffn_up, w_ffn_down, final_g):
    b, s, d = x.shape
    assert w_in.shape[0] == 1, "single-layer block"
    x2 = x.reshape(b * s, d)
    pos2 = positions.reshape(b * s, 1)
    ws_all = jnp.transpose(w_spatial[0], (1, 0, 2)).reshape(SGU_CHUNK, SGU_GROUPS * SGU_CHUNK)
    bs_full = jnp.repeat(jnp.transpose(b_spatial[0]), SGU_GROUP_DIM, axis=1)
    *qkvs, sgu, gates = _in_proj(
        x2, pos2, norm1_g[0][None, :], _pair_slab_order(w_in[0]).astype(BF16), _rope_freq_row(),
        sgu_ln_g[0][None, :], sgu_ln_b[0][None, :], ws_all.astype(BF16), bs_full, b)
    os_, lses = [], []
    for qkv, dilation in zip(qkvs, GROUP_DILATIONS):
        length = s // dilation
        o, lse = _attn_group(qkv.reshape(b * dilation, length, 3 * ATTN_WIDTH))
        os_.append(o.reshape(b, dilation, length, ATTN_WIDTH))
        lses.append(lse.reshape(b, dilation, length, LANES))
    out = _post(x2, os_, lses, sgu, gates,
                w_proj_attn[0].astype(BF16), w_proj_sgu[0].astype(BF16), w_out[0].astype(BF16),
                norm2_g[0][None, :], w_ffn_gate[0].astype(BF16), w_ffn_up[0].astype(BF16),
                w_ffn_down[0].astype(BF16), final_g[None, :], b)
    return out.reshape(b, s, d)
```

```python
import jax
import jax.numpy as jnp
import numpy as np
from jax import lax
from jax.experimental import pallas as pl
from jax.experimental.pallas import tpu as pltpu

D_MODEL = 1024
HEAD_DIM = 64
HEADS = 8
GROUP_DILATIONS = (1, 4, 16)
N_DIL = len(GROUP_DILATIONS)
SPAN = 128
ATTN_WIDTH = HEADS * HEAD_DIM
ROPE_DIM = HEAD_DIM // 4
ROPE_HALF = ROPE_DIM // 2
ROPE_THETA = 500000.0
SGU_CHUNK = 128
SGU_GROUPS = 8
SGU_WIDTH = D_MODEL // 2
SGU_GROUP_DIM = SGU_WIDTH // SGU_GROUPS
D_FF = 2816
QKV_COLS = 3 * N_DIL * ATTN_WIDTH
UV_OFF = QKV_COLS
GATE_OFF = QKV_COLS + 2 * SGU_WIDTH
IN_COLS = GATE_OFF + 2 * D_MODEL
EPS = 1e-6

LANES = 128
V7X_VMEM_BYTES = 64 * 1024 * 1024
VMEM_LIMIT = V7X_VMEM_BYTES - 8 * 1024 * 1024

IN_TM = 512
POST_TM = 512
ATTN_QB = 512
COL_CHUNK = 512

BF16 = jnp.bfloat16
F32 = jnp.float32


def _resident(shape):
    nd = len(shape)
    return pl.BlockSpec(shape, lambda *_: (0,) * nd, pipeline_mode=pl.Buffered(1))


def _dot(a, b):
    return jnp.dot(a, b, preferred_element_type=F32)


def _in_proj_kernel(x_ref, pos_ref, g1_ref, w_ref, freq_ref, lng_ref, lnb_ref, ws_ref, bs_ref,
                    qkv0_ref, qkv1_ref, qkv2_ref, sgu_ref, gates_ref, st_ref):
    tm = x_ref.shape[0]
    x = x_ref[...]
    h = (x * lax.rsqrt(jnp.mean(x * x, axis=-1, keepdims=True) + EPS) * g1_ref[...]).astype(BF16)

    ang = pos_ref[...].astype(F32) * freq_ref[...]
    cos = jnp.cos(ang)
    sin = jnp.sin(ang)
    lane = lax.broadcasted_iota(jnp.int32, (1, LANES), 1) % HEAD_DIM
    sin_up = jnp.where((lane >= ROPE_HALF) & (lane < ROPE_DIM), sin, 0.0)
    sin_dn = jnp.where(lane < ROPE_HALF, -sin, 0.0)

    qkv_refs = (qkv0_ref, qkv1_ref, qkv2_ref)
    def project(col, z):
        w = w_ref[:, col:col + COL_CHUNK]
        if z is not None:
            parts = []
            for k0 in range(0, D_MODEL, 256):
                parts += [w[k0:k0 + 16] + z, w[k0 + 16:k0 + 256]]
            w = jnp.concatenate(parts, axis=0)
        return _dot(h, w)

    def fold(tb):
        return jnp.sum(tb.reshape(tb.shape[0] // 16, 16, LANES), axis=0)

    def zero_from(folds):
        return jnp.concatenate([f - f for f in folds], axis=1)

    def finish_qkv(which, g, y):
        dilation = GROUP_DILATIONS[g]
        out_ref = qkv_refs[g]
        o0 = which * ATTN_WIDTH
        folds = []
        scale = HEAD_DIM ** -0.5 * np.log2(np.e) if which == 0 else 1.0
        for l in range(0, ATTN_WIDTH, LANES):
            t = y[:, l:l + LANES]
            if which < 2:
                t = (t * cos + pltpu.roll(t, ROPE_HALF, 1) * sin_up
                     + pltpu.roll(t, LANES - ROPE_HALF, 1) * sin_dn) * scale
            if dilation == 1:
                tb = t.astype(BF16)
                out_ref[0, :, o0 + l:o0 + l + LANES] = tb
                folds.append(fold(tb))
            else:
                st_ref[l // LANES] = t
        if dilation > 1:
            for r in range(dilation):
                for l in range(0, ATTN_WIDTH, LANES):
                    piece = st_ref[l // LANES, pl.ds(r, tm // dilation, stride=dilation), :].astype(BF16)
                    out_ref[r, :, o0 + l:o0 + l + LANES] = piece
                    if r == 0:
                        folds.append(fold(piece))
                    else:
                        folds[l // LANES] = folds[l // LANES] + fold(piece)
        return zero_from(folds)

    def gelu(y):
        return 0.5 * y * (1.0 + lax.erf(y * np.float32(np.sqrt(0.5))))

    held = {}

    def finish_u(y):
        ub = gelu(y)
        held["u"] = ub
        return zero_from([fold(ub[:, l:l + LANES].astype(BF16)) for l in range(0, SGU_WIDTH, LANES)])

    def finish_v(y):
        u = held["u"]
        v = gelu(y)
        mu = jnp.mean(v, axis=-1, keepdims=True)
        vc = v - mu
        v = vc * lax.rsqrt(jnp.mean(vc * vc, axis=-1, keepdims=True) + EPS) * lng_ref[...] + lnb_ref[...]
        v = v.astype(BF16)
        row = lax.broadcasted_iota(jnp.int32, (SGU_CHUNK, SGU_GROUPS * SGU_CHUNK), 0)
        col = lax.broadcasted_iota(jnp.int32, (SGU_CHUNK, SGU_GROUPS * SGU_CHUNK), 1) % SGU_CHUNK
        w_causal = jnp.where(col <= row, ws_ref[...], jnp.zeros((), BF16))
        lane_group = lax.broadcasted_iota(jnp.int32, (1, SGU_WIDTH), 1) // SGU_GROUP_DIM
        folds = None
        for r0 in range(0, tm, SGU_CHUNK):
            vch = v[r0:r0 + SGU_CHUNK]
            v_bd = jnp.concatenate(
                [jnp.where(lane_group == g, vch, jnp.zeros((), BF16)) for g in range(SGU_GROUPS)], axis=0)
            mixed = _dot(w_causal, v_bd) + bs_ref[...]
            sb = (u[r0:r0 + SGU_CHUNK] * mixed).astype(BF16)
            sgu_ref[r0:r0 + SGU_CHUNK, :] = sb
            part = [fold(sb[:, l:l + LANES]) for l in range(0, SGU_WIDTH, LANES)]
            folds = part if folds is None else [a + b for a, b in zip(folds, part)]
        return zero_from(folds)

    def finish_gate(col, y):
        sb = jax.nn.sigmoid(y).astype(BF16)
        gates_ref[:, col - GATE_OFF:col - GATE_OFF + COL_CHUNK] = sb
        return zero_from([fold(sb[:, l:l + LANES]) for l in range(0, COL_CHUNK, LANES)])

    chunks = [((which * N_DIL + g) * ATTN_WIDTH, (lambda y, which=which, g=g: finish_qkv(which, g, y)))
              for which in range(3) for g in range(N_DIL)]
    chunks += [(UV_OFF, finish_u), (UV_OFF + SGU_WIDTH, finish_v)]
    chunks += [(c, (lambda y, c=c: finish_gate(c, y))) for c in range(GATE_OFF, IN_COLS, COL_CHUNK)]

    LAG = 3
    zeros = []
    y_next = project(chunks[0][0], None)
    for idx, (_, finish) in enumerate(chunks):
        y = y_next
        if idx + 1 < len(chunks):
            y_next = project(chunks[idx + 1][0], zeros[idx - LAG] if idx >= LAG else None)
        zeros.append(finish(y))


def _in_proj(x2, pos2, g1, w_in, freq, lng, lnb, ws_all, bs_full, batch):
    n = x2.shape[0]
    seq = n // batch
    tiles = seq // IN_TM
    row = lambda width: pl.BlockSpec((IN_TM, width), lambda i: (i, 0))

    def strided(d):
        return pl.BlockSpec((None, d, IN_TM // d, 3 * ATTN_WIDTH),
                            lambda i: (i // tiles, 0, i % tiles, 0))

    return pl.pallas_call(
        _in_proj_kernel,
        grid=(n // IN_TM,),
        in_specs=[
            row(D_MODEL), row(1),
            _resident((1, D_MODEL)), _resident((D_MODEL, IN_COLS)), _resident((1, LANES)),
            _resident((1, SGU_WIDTH)), _resident((1, SGU_WIDTH)),
            _resident((SGU_CHUNK, SGU_GROUPS * SGU_CHUNK)), _resident((SGU_CHUNK, SGU_WIDTH)),
        ],
        out_specs=[strided(d) for d in GROUP_DILATIONS] + [row(SGU_WIDTH), row(2 * D_MODEL)],
        out_shape=[jax.ShapeDtypeStruct((batch, d, seq // d, 3 * ATTN_WIDTH), BF16)
                   for d in GROUP_DILATIONS] + [
            jax.ShapeDtypeStruct((n, SGU_WIDTH), BF16),
            jax.ShapeDtypeStruct((n, 2 * D_MODEL), BF16),
        ],
        scratch_shapes=[pltpu.VMEM((ATTN_WIDTH // LANES, IN_TM, LANES), F32)],
        compiler_params=pltpu.CompilerParams(
            dimension_semantics=("arbitrary",), vmem_limit_bytes=VMEM_LIMIT),
        name="in_proj",
    )(x2, pos2, g1, w_in, freq, lng, lnb, ws_all, bs_full)


def _attn_kernel(q_ref, k_ref, kp_ref, v_ref, vp_ref, o_ref, stats_ref, bias_ref):
    first = pl.program_id(1) == 0
    qi = lax.broadcasted_iota(jnp.int32, (SPAN, 2 * SPAN), 0)
    kj = lax.broadcasted_iota(jnp.int32, (SPAN, 2 * SPAN), 1)
    band = (kj >= qi) & (kj <= qi + SPAN)
    first_lo = jnp.where(first, SPAN, 0)
    lane = lax.broadcasted_iota(jnp.int32, (1, LANES), 1)
    lo_head = lane < HEAD_DIM
    zero = jnp.zeros((), BF16)
    bias_ref[0] = jnp.where(band & (kj >= first_lo), 0.0, -jnp.inf)
    bias_ref[1] = jnp.where(band, 0.0, -jnp.inf)

    for j in range(0, ATTN_QB, SPAN):
        bias = bias_ref.at[0 if j == 0 else 1]
        stats_ref[0, j:j + SPAN, :] = jnp.zeros((SPAN, LANES), F32)
        for p in range(0, ATTN_WIDTH, LANES):
            q = q_ref[0, j:j + SPAN, p:p + LANES]
            if j == 0:
                k = jnp.concatenate([kp_ref[0, :, p:p + LANES], k_ref[0, 0:SPAN, p:p + LANES]], axis=0)
                v = jnp.concatenate([vp_ref[0, :, p:p + LANES], v_ref[0, 0:SPAN, p:p + LANES]], axis=0)
            else:
                k = k_ref[0, j - SPAN:j + SPAN, p:p + LANES]
                v = v_ref[0, j - SPAN:j + SPAN, p:p + LANES]
            qq = jnp.concatenate([jnp.where(lo_head, q, zero), jnp.where(lo_head, zero, q)], axis=0)
            s = lax.dot_general(qq, k, (((1,), (1,)), ((), ())), preferred_element_type=F32)
            s = s + jnp.concatenate([bias[...], bias[...]], axis=0)
            m = jnp.max(s, axis=-1, keepdims=True)
            e = jnp.exp2(s - m)
            den = jnp.sum(e, axis=-1, keepdims=True)
            res = _dot(e.astype(BF16), v)
            for half in range(2):
                head = p // HEAD_DIM + half
                stats_ref[0, j:j + SPAN, head:head + 1] = m[half * SPAN:(half + 1) * SPAN]
                stats_ref[0, j:j + SPAN, HEADS + head:HEADS + head + 1] = den[half * SPAN:(half + 1) * SPAN]
            o_ref[0, j:j + SPAN, p:p + LANES] = jnp.where(lo_head, res[:SPAN], res[SPAN:]).astype(BF16)


def _attn_group(qkv):
    rows, length, _ = qkv.shape
    assert length % ATTN_QB == 0
    nq = length // ATTN_QB
    sub = ATTN_QB // SPAN

    def cur(which):
        return pl.BlockSpec((1, ATTN_QB, ATTN_WIDTH), lambda r, i: (r, i, which))

    def prev(which):
        return pl.BlockSpec((1, SPAN, ATTN_WIDTH),
                            lambda r, i: (r, jnp.maximum(i * sub - 1, 0), which))

    out_spec = lambda width: pl.BlockSpec((1, ATTN_QB, width), lambda r, i: (r, i, 0))
    return pl.pallas_call(
        _attn_kernel,
        grid=(rows, nq),
        in_specs=[cur(0), cur(1), prev(1), cur(2), prev(2)],
        out_specs=[out_spec(ATTN_WIDTH), out_spec(LANES)],
        out_shape=[
            jax.ShapeDtypeStruct((rows, length, ATTN_WIDTH), BF16),
            jax.ShapeDtypeStruct((rows, length, LANES), F32),
        ],
        scratch_shapes=[pltpu.VMEM((2, SPAN, 2 * SPAN), F32)],
        compiler_params=pltpu.CompilerParams(
            dimension_semantics=("arbitrary", "arbitrary"), vmem_limit_bytes=VMEM_LIMIT),
        name=f"attn_len{length}",
    )(qkv, qkv, qkv, qkv, qkv)


FF_CHUNKS = ((0, 1024), (1024, 2048), (2048, D_FF))


def _post_kernel(x_ref, o0_ref, o1_ref, o2_ref, l0_ref, l1_ref, l2_ref, sgu_ref, gates_ref,
                 wpa_ref, wps_ref, wo_ref, g2_ref, wg_ref, wu_ref, wd_ref, gf_ref, ex_ref,
                 out_ref, ff_ref, tok_l, tok_o):
    tm = x_ref.shape[0]

    def token_major(ref, scratch):
        dilation, _, width = ref.shape
        if dilation == 1:
            return ref[0].astype(F32)
        for r in range(dilation):
            for l in range(0, width, LANES):
                scratch[l // LANES, pl.ds(r, tm // dilation, stride=dilation), :] = (
                    ref[r, :, l:l + LANES].astype(F32))
        return jnp.concatenate([scratch[i] for i in range(width // LANES)], axis=1)

    def per_head_to_lanes(a):
        hi = a.astype(BF16)
        lo = (a - hi.astype(F32)).astype(BF16)
        return _dot(jnp.concatenate([hi, lo], axis=1), ex_ref[...])

    st = (token_major(l0_ref, None), token_major(l1_ref, tok_l.at[0]), token_major(l2_ref, tok_l.at[1]))
    is_head = lax.broadcasted_iota(jnp.int32, (1, LANES), 1) < HEADS
    mx = jnp.maximum(jnp.maximum(st[0], st[1]), st[2])
    t = [jnp.exp2(s - mx) for s in st]
    den = [pltpu.roll(s, LANES - HEADS, 1) for s in st]
    total = jnp.where(is_head, den[0] * t[0] + den[1] * t[1] + den[2] * t[2], 1.0)
    inv = 1.0 / total
    w = [jnp.where(is_head, tg * inv, 0.0) for tg in t]
    attn = (per_head_to_lanes(w[0]) * token_major(o0_ref, None)
            + per_head_to_lanes(w[1]) * token_major(o1_ref, tok_o.at[0])
            + per_head_to_lanes(w[2]) * token_major(o2_ref, tok_o.at[1])).astype(BF16)

    gate_a = gates_ref[:, 0:D_MODEL].astype(F32)
    gate_b = gates_ref[:, D_MODEL:2 * D_MODEL].astype(F32)
    merged = gate_a * _dot(attn, wpa_ref[...]) + gate_b * _dot(sgu_ref[...], wps_ref[...])
    x1 = x_ref[...] + _dot(merged.astype(BF16), wo_ref[...])

    h2 = (x1 * lax.rsqrt(jnp.mean(x1 * x1, axis=-1, keepdims=True) + EPS) * g2_ref[...]).astype(BF16)
    for c0, c1 in FF_CHUNKS:
        gate = _dot(h2, wg_ref[:, c0:c1])
        up = _dot(h2, wu_ref[:, c0:c1])
        ff_ref[:, c0:c1] = (gate * jax.nn.sigmoid(gate) * up).astype(BF16)
    x2 = x1 + _dot(ff_ref[...], wd_ref[...])
    out_ref[...] = x2 * lax.rsqrt(jnp.mean(x2 * x2, axis=-1, keepdims=True) + EPS) * gf_ref[...]


def _post(x2, os_, lses, sgu, gates, wpa, wps, wo, g2, wg, wu, wd, gf, batch):
    n = x2.shape[0]
    seq = n // batch
    tiles = seq // POST_TM
    row = lambda width: pl.BlockSpec((POST_TM, width), lambda i: (i, 0))

    def strided(d, width):
        return pl.BlockSpec((None, d, POST_TM // d, width),
                            lambda i: (i // tiles, 0, i % tiles, 0))

    head_of_lane = np.arange(ATTN_WIDTH) // HEAD_DIM
    expand = (np.arange(2 * LANES)[:, None] % LANES == head_of_lane[None, :])
    expand = jnp.asarray(expand, dtype=BF16)

    return pl.pallas_call(
        _post_kernel,
        grid=(n // POST_TM,),
        in_specs=[row(D_MODEL)] + [strided(d, ATTN_WIDTH) for d in GROUP_DILATIONS]
        + [strided(d, LANES) for d in GROUP_DILATIONS] + [
            row(SGU_WIDTH), row(2 * D_MODEL),
            _resident((ATTN_WIDTH, D_MODEL)), _resident((SGU_WIDTH, D_MODEL)),
            _resident((D_MODEL, D_MODEL)), _resident((1, D_MODEL)),
            _resident((D_MODEL, D_FF)), _resident((D_MODEL, D_FF)), _resident((D_FF, D_MODEL)),
            _resident((1, D_MODEL)), _resident((2 * LANES, ATTN_WIDTH)),
        ],
        out_specs=row(D_MODEL),
        out_shape=jax.ShapeDtypeStruct((n, D_MODEL), F32),
        scratch_shapes=[pltpu.VMEM((POST_TM, D_FF), BF16),
                        pltpu.VMEM((2, 1, POST_TM, LANES), F32),
                        pltpu.VMEM((2, ATTN_WIDTH // LANES, POST_TM, LANES), F32)],
        compiler_params=pltpu.CompilerParams(
            dimension_semantics=("arbitrary",), vmem_limit_bytes=VMEM_LIMIT),
        name="post",
    )(x2, *os_, *lses, sgu, gates, wpa, wps, wo, g2, wg, wu, wd, gf, expand)


def _rope_freq_row():
    inv_freq = ROPE_THETA ** (-jnp.arange(0, ROPE_DIM, 2, dtype=F32) / ROPE_DIM)
    per_head = jnp.concatenate([inv_freq, inv_freq, jnp.zeros((HEAD_DIM - ROPE_DIM,), F32)])
    return jnp.tile(per_head, LANES // HEAD_DIM)[None, :]


def kernel(x, positions, norm1_g, w_in, sgu_ln_g, sgu_ln_b, w_spatial, b_spatial, w_proj_attn,
           w_proj_sgu, w_out, norm2_g, w_ffn_gate, w_ffn_up, w_ffn_down, final_g):
    b, s, d = x.shape
    assert w_in.shape[0] == 1, "single-layer block"
    x2 = x.reshape(b * s, d)
    pos2 = positions.reshape(b * s, 1)
    ws_all = jnp.transpose(w_spatial[0], (1, 0, 2)).reshape(SGU_CHUNK, SGU_GROUPS * SGU_CHUNK)
    bs_full = jnp.repeat(jnp.transpose(b_spatial[0]), SGU_GROUP_DIM, axis=1)
    *qkvs, sgu, gates = _in_proj(
        x2, pos2, norm1_g[0][None, :], w_in[0].astype(BF16), _rope_freq_row(),
        sgu_ln_g[0][None, :], sgu_ln_b[0][None, :], ws_all.astype(BF16), bs_full, b)
    os_, lses = [], []
    for qkv, dilation in zip(qkvs, GROUP_DILATIONS):
        length = s // dilation
        o, lse = _attn_group(qkv.reshape(b * dilation, length, 3 * ATTN_WIDTH))
        os_.append(o.reshape(b, dilation, length, ATTN_WIDTH))
        lses.append(lse.reshape(b, dilation, length, LANES))
    out = _post(x2, os_, lses, sgu, gates,
                w_proj_attn[0].astype(BF16), w_proj_sgu[0].astype(BF16), w_out[0].astype(BF16),
                norm2_g[0][None, :], w_ffn_gate[0].astype(BF16), w_ffn_up[0].astype(BF16),
                w_ffn_down[0].astype(BF16), final_g[None, :], b)
    return out.reshape(b, s, d)
```

```python
import jax
import jax.numpy as jnp
import numpy as np
from jax import lax
from jax.experimental import pallas as pl
from jax.experimental.pallas import tpu as pltpu

D_MODEL = 1024
HEAD_DIM = 64
HEADS = 8
GROUP_DILATIONS = (1, 4, 16)
N_DIL = len(GROUP_DILATIONS)
SPAN = 128
ATTN_WIDTH = HEADS * HEAD_DIM
ROPE_DIM = HEAD_DIM // 4
ROPE_HALF = ROPE_DIM // 2
ROPE_THETA = 500000.0
SGU_CHUNK = 128
SGU_GROUPS = 8
SGU_WIDTH = D_MODEL // 2
SGU_GROUP_DIM = SGU_WIDTH // SGU_GROUPS
D_FF = 2816
QKV_COLS = 3 * N_DIL * ATTN_WIDTH
UV_OFF = QKV_COLS
GATE_OFF = QKV_COLS + 2 * SGU_WIDTH
IN_COLS = GATE_OFF + 2 * D_MODEL
EPS = 1e-6

LANES = 128
V7X_VMEM_BYTES = 64 * 1024 * 1024
VMEM_LIMIT = V7X_VMEM_BYTES - 8 * 1024 * 1024

IN_TM = 512
POST_TM = 512
ATTN_QB = 512
COL_CHUNK = 512

BF16 = jnp.bfloat16
F32 = jnp.float32


def _resident(shape):
    nd = len(shape)
    return pl.BlockSpec(shape, lambda *_: (0,) * nd, pipeline_mode=pl.Buffered(1))


def _dot(a, b):
    return jnp.dot(a, b, preferred_element_type=F32)


def _in_proj_kernel(x_ref, pos_ref, g1_ref, w_ref, freq_ref, lng_ref, lnb_ref, ws_ref, bs_ref,
                    qkv0_ref, qkv1_ref, qkv2_ref, sgu_ref, gates_ref, st_ref):
    tm = x_ref.shape[0]
    x = x_ref[...]
    h = (x * lax.rsqrt(jnp.mean(x * x, axis=-1, keepdims=True) + EPS) * g1_ref[...]).astype(BF16)

    ang = pos_ref[...].astype(F32) * freq_ref[...]
    cos = jnp.cos(ang)
    sin = jnp.sin(ang)
    lane = lax.broadcasted_iota(jnp.int32, (1, LANES), 1) % HEAD_DIM
    sin_up = jnp.where((lane >= ROPE_HALF) & (lane < ROPE_DIM), sin, 0.0)
    sin_dn = jnp.where(lane < ROPE_HALF, -sin, 0.0)

    qkv_refs = (qkv0_ref, qkv1_ref, qkv2_ref)
    def project(col, z):
        w = w_ref[:, col:col + COL_CHUNK]
        if z is not None:
            parts = []
            for k0 in range(0, D_MODEL, 256):
                parts += [w[k0:k0 + 16] + z, w[k0 + 16:k0 + 256]]
            w = jnp.concatenate(parts, axis=0)
        return _dot(h, w)

    def fold(tb):
        return jnp.sum(tb.reshape(tb.shape[0] // 16, 16, LANES), axis=0)

    def zero_from(folds):
        return jnp.concatenate([jnp.minimum(jnp.maximum(f, 0), 0) for f in folds], axis=1)

    def finish_qkv(which, g, y):
        dilation = GROUP_DILATIONS[g]
        out_ref = qkv_refs[g]
        o0 = which * ATTN_WIDTH
        folds = []
        scale = HEAD_DIM ** -0.5 * np.log2(np.e) if which == 0 else 1.0
        for l in range(0, ATTN_WIDTH, LANES):
            t = y[:, l:l + LANES]
            if which < 2:
                t = (t * cos + pltpu.roll(t, ROPE_HALF, 1) * sin_up
                     + pltpu.roll(t, LANES - ROPE_HALF, 1) * sin_dn) * scale
            if dilation == 1:
                tb = t.astype(BF16)
                out_ref[0, :, o0 + l:o0 + l + LANES] = tb
                folds.append(fold(tb))
            else:
                st_ref[l // LANES] = t
        if dilation > 1:
            for r in range(dilation):
                for l in range(0, ATTN_WIDTH, LANES):
                    piece = st_ref[l // LANES, pl.ds(r, tm // dilation, stride=dilation), :].astype(BF16)
                    out_ref[r, :, o0 + l:o0 + l + LANES] = piece
                    if r == 0:
                        folds.append(fold(piece))
                    else:
                        folds[l // LANES] = folds[l // LANES] + fold(piece)
        return zero_from(folds)

    def gelu(y):
        return 0.5 * y * (1.0 + lax.erf(y * np.float32(np.sqrt(0.5))))

    held = {}

    def finish_u(y):
        ub = gelu(y)
        held["u"] = ub
        return zero_from([fold(ub[:, l:l + LANES].astype(BF16)) for l in range(0, SGU_WIDTH, LANES)])

    def finish_v(y):
        u = held["u"]
        v = gelu(y)
        mu = jnp.mean(v, axis=-1, keepdims=True)
        vc = v - mu
        v = vc * lax.rsqrt(jnp.mean(vc * vc, axis=-1, keepdims=True) + EPS) * lng_ref[...] + lnb_ref[...]
        v = v.astype(BF16)
        row = lax.broadcasted_iota(jnp.int32, (SGU_CHUNK, SGU_GROUPS * SGU_CHUNK), 0)
        col = lax.broadcasted_iota(jnp.int32, (SGU_CHUNK, SGU_GROUPS * SGU_CHUNK), 1) % SGU_CHUNK
        w_causal = jnp.where(col <= row, ws_ref[...], jnp.zeros((), BF16))
        lane_group = lax.broadcasted_iota(jnp.int32, (1, SGU_WIDTH), 1) // SGU_GROUP_DIM
        folds = None
        for r0 in range(0, tm, SGU_CHUNK):
            vch = v[r0:r0 + SGU_CHUNK]
            v_bd = jnp.concatenate(
                [jnp.where(lane_group == g, vch, jnp.zeros((), BF16)) for g in range(SGU_GROUPS)], axis=0)
            mixed = _dot(w_causal, v_bd) + bs_ref[...]
            sb = (u[r0:r0 + SGU_CHUNK] * mixed).astype(BF16)
            sgu_ref[r0:r0 + SGU_CHUNK, :] = sb
            part = [fold(sb[:, l:l + LANES]) for l in range(0, SGU_WIDTH, LANES)]
            folds = part if folds is None else [a + b for a, b in zip(folds, part)]
        return zero_from(folds)

    def finish_gate(col, y):
        sb = jax.nn.sigmoid(y).astype(BF16)
        gates_ref[:, col - GATE_OFF:col - GATE_OFF + COL_CHUNK] = sb
        return zero_from([fold(sb[:, l:l + LANES]) for l in range(0, COL_CHUNK, LANES)])

    qkv = [((which * N_DIL + g) * ATTN_WIDTH, (lambda y, which=which, g=g: finish_qkv(which, g, y)))
           for which in range(3) for g in range(N_DIL)]
    gate = [(c, (lambda y, c=c: finish_gate(c, y))) for c in range(GATE_OFF, IN_COLS, COL_CHUNK)]
    sgu = [(UV_OFF, finish_u), (UV_OFF + SGU_WIDTH, finish_v)]
    rot, plain = qkv[:2 * N_DIL], qkv[2 * N_DIL:] + gate
    chunks = [c for pair in zip(rot, plain) for c in pair] + plain[len(rot):] + sgu

    LAG = 3
    zeros = []
    y_next = project(chunks[0][0], None)
    for idx, (_, finish) in enumerate(chunks):
        y = y_next
        if idx + 1 < len(chunks):
            y_next = project(chunks[idx + 1][0], zeros[idx - LAG] if idx >= LAG else None)
        zeros.append(finish(y))


def _in_proj(x2, pos2, g1, w_in, freq, lng, lnb, ws_all, bs_full, batch):
    n = x2.shape[0]
    seq = n // batch
    tiles = seq // IN_TM
    row = lambda width: pl.BlockSpec((IN_TM, width), lambda i: (i, 0))

    def strided(d):
        return pl.BlockSpec((None, d, IN_TM // d, 3 * ATTN_WIDTH),
                            lambda i: (i // tiles, 0, i % tiles, 0))

    return pl.pallas_call(
        _in_proj_kernel,
        grid=(n // IN_TM,),
        in_specs=[
            row(D_MODEL), row(1),
            _resident((1, D_MODEL)), _resident((D_MODEL, IN_COLS)), _resident((1, LANES)),
            _resident((1, SGU_WIDTH)), _resident((1, SGU_WIDTH)),
            _resident((SGU_CHUNK, SGU_GROUPS * SGU_CHUNK)), _resident((SGU_CHUNK, SGU_WIDTH)),
        ],
        out_specs=[strided(d) for d in GROUP_DILATIONS] + [row(SGU_WIDTH), row(2 * D_MODEL)],
        out_shape=[jax.ShapeDtypeStruct((batch, d, seq // d, 3 * ATTN_WIDTH), BF16)
                   for d in GROUP_DILATIONS] + [
            jax.ShapeDtypeStruct((n, SGU_WIDTH), BF16),
            jax.ShapeDtypeStruct((n, 2 * D_MODEL), BF16),
        ],
        scratch_shapes=[pltpu.VMEM((ATTN_WIDTH // LANES, IN_TM, LANES), F32)],
        compiler_params=pltpu.CompilerParams(
            dimension_semantics=("arbitrary",), vmem_limit_bytes=VMEM_LIMIT),
        name="in_proj",
    )(x2, pos2, g1, w_in, freq, lng, lnb, ws_all, bs_full)


def _attn_kernel(q_ref, k_ref, kp_ref, v_ref, vp_ref, o_ref, stats_ref, bias_ref):
    first = pl.program_id(1) == 0
    qi = lax.broadcasted_iota(jnp.int32, (SPAN, 2 * SPAN), 0)
    kj = lax.broadcasted_iota(jnp.int32, (SPAN, 2 * SPAN), 1)
    band = (kj >= qi) & (kj <= qi + SPAN)
    first_lo = jnp.where(first, SPAN, 0)
    lane = lax.broadcasted_iota(jnp.int32, (1, LANES), 1)
    lo_head = lane < HEAD_DIM
    zero = jnp.zeros((), BF16)
    bias_ref[0] = jnp.where(band & (kj >= first_lo), 0.0, -jnp.inf)
    bias_ref[1] = jnp.where(band, 0.0, -jnp.inf)

    for j in range(0, ATTN_QB, SPAN):
        bias = bias_ref.at[0 if j == 0 else 1]
        stats_ref[0, j:j + SPAN, :] = jnp.zeros((SPAN, LANES), F32)
        for p in range(0, ATTN_WIDTH, LANES):
            q = q_ref[0, j:j + SPAN, p:p + LANES]
            if j == 0:
                k = jnp.concatenate([kp_ref[0, :, p:p + LANES], k_ref[0, 0:SPAN, p:p + LANES]], axis=0)
                v = jnp.concatenate([vp_ref[0, :, p:p + LANES], v_ref[0, 0:SPAN, p:p + LANES]], axis=0)
            else:
                k = k_ref[0, j - SPAN:j + SPAN, p:p + LANES]
                v = v_ref[0, j - SPAN:j + SPAN, p:p + LANES]
            qq = jnp.concatenate([jnp.where(lo_head, q, zero), jnp.where(lo_head, zero, q)], axis=0)
            s = lax.dot_general(qq, k, (((1,), (1,)), ((), ())), preferred_element_type=F32)
            s = s + jnp.concatenate([bias[...], bias[...]], axis=0)
            m = jnp.max(s, axis=-1, keepdims=True)
            e = jnp.exp2(s - m)
            den = jnp.sum(e, axis=-1, keepdims=True)
            res = _dot(e.astype(BF16), v)
            for half in range(2):
                head = p // HEAD_DIM + half
                stats_ref[0, j:j + SPAN, head:head + 1] = m[half * SPAN:(half + 1) * SPAN]
                stats_ref[0, j:j + SPAN, HEADS + head:HEADS + head + 1] = den[half * SPAN:(half + 1) * SPAN]
            o_ref[0, j:j + SPAN, p:p + LANES] = jnp.where(lo_head, res[:SPAN], res[SPAN:]).astype(BF16)


def _attn_group(qkv):
    rows, length, _ = qkv.shape
    assert length % ATTN_QB == 0
    nq = length // ATTN_QB
    sub = ATTN_QB // SPAN

    def cur(which):
        return pl.BlockSpec((1, ATTN_QB, ATTN_WIDTH), lambda r, i: (r, i, which))

    def prev(which):
        return pl.BlockSpec((1, SPAN, ATTN_WIDTH),
                            lambda r, i: (r, jnp.maximum(i * sub - 1, 0), which))

    out_spec = lambda width: pl.BlockSpec((1, ATTN_QB, width), lambda r, i: (r, i, 0))
    return pl.pallas_call(
        _attn_kernel,
        grid=(rows, nq),
        in_specs=[cur(0), cur(1), prev(1), cur(2), prev(2)],
        out_specs=[out_spec(ATTN_WIDTH), out_spec(LANES)],
        out_shape=[
            jax.ShapeDtypeStruct((rows, length, ATTN_WIDTH), BF16),
            jax.ShapeDtypeStruct((rows, length, LANES), F32),
        ],
        scratch_shapes=[pltpu.VMEM((2, SPAN, 2 * SPAN), F32)],
        compiler_params=pltpu.CompilerParams(
            dimension_semantics=("arbitrary", "arbitrary"), vmem_limit_bytes=VMEM_LIMIT),
        name=f"attn_len{length}",
    )(qkv, qkv, qkv, qkv, qkv)


FF_CHUNKS = ((0, 1024), (1024, 2048), (2048, D_FF))


def _post_kernel(x_ref, o0_ref, o1_ref, o2_ref, l0_ref, l1_ref, l2_ref, sgu_ref, gates_ref,
                 wpa_ref, wps_ref, wo_ref, g2_ref, wg_ref, wu_ref, wd_ref, gf_ref, ex_ref,
                 out_ref, ff_ref, tok_l, tok_o):
    tm = x_ref.shape[0]

    def token_major(ref, scratch):
        dilation, _, width = ref.shape
        if dilation == 1:
            return ref[0].astype(F32)
        for r in range(dilation):
            for l in range(0, width, LANES):
                scratch[l // LANES, pl.ds(r, tm // dilation, stride=dilation), :] = (
                    ref[r, :, l:l + LANES].astype(F32))
        return jnp.concatenate([scratch[i] for i in range(width // LANES)], axis=1)

    def per_head_to_lanes(a):
        hi = a.astype(BF16)
        lo = (a - hi.astype(F32)).astype(BF16)
        return _dot(jnp.concatenate([hi, lo], axis=1), ex_ref[...])

    st = (token_major(l0_ref, None), token_major(l1_ref, tok_l.at[0]), token_major(l2_ref, tok_l.at[1]))
    is_head = lax.broadcasted_iota(jnp.int32, (1, LANES), 1) < HEADS
    mx = jnp.maximum(jnp.maximum(st[0], st[1]), st[2])
    t = [jnp.exp2(s - mx) for s in st]
    den = [pltpu.roll(s, LANES - HEADS, 1) for s in st]
    total = jnp.where(is_head, den[0] * t[0] + den[1] * t[1] + den[2] * t[2], 1.0)
    inv = 1.0 / total
    w = [jnp.where(is_head, tg * inv, 0.0) for tg in t]
    attn = (per_head_to_lanes(w[0]) * token_major(o0_ref, None)
            + per_head_to_lanes(w[1]) * token_major(o1_ref, tok_o.at[0])
            + per_head_to_lanes(w[2]) * token_major(o2_ref, tok_o.at[1])).astype(BF16)

    gate_a = gates_ref[:, 0:D_MODEL].astype(F32)
    gate_b = gates_ref[:, D_MODEL:2 * D_MODEL].astype(F32)
    merged = gate_a * _dot(attn, wpa_ref[...]) + gate_b * _dot(sgu_ref[...], wps_ref[...])
    x1 = x_ref[...] + _dot(merged.astype(BF16), wo_ref[...])

    h2 = (x1 * lax.rsqrt(jnp.mean(x1 * x1, axis=-1, keepdims=True) + EPS) * g2_ref[...]).astype(BF16)
    for c0, c1 in FF_CHUNKS:
        gate = _dot(h2, wg_ref[:, c0:c1])
        up = _dot(h2, wu_ref[:, c0:c1])
        ff_ref[:, c0:c1] = (gate * jax.nn.sigmoid(gate) * up).astype(BF16)
    x2 = x1 + _dot(ff_ref[...], wd_ref[...])
    out_ref[...] = x2 * lax.rsqrt(jnp.mean(x2 * x2, axis=-1, keepdims=True) + EPS) * gf_ref[...]


def _post(x2, os_, lses, sgu, gates, wpa, wps, wo, g2, wg, wu, wd, gf, batch):
    n = x2.shape[0]
    seq = n // batch
    tiles = seq // POST_TM
    row = lambda width: pl.BlockSpec((POST_TM, width), lambda i: (i, 0))

    def strided(d, width):
        return pl.BlockSpec((None, d, POST_TM // d, width),
                            lambda i: (i // tiles, 0, i % tiles, 0))

    head_of_lane = np.arange(ATTN_WIDTH) // HEAD_DIM
    expand = (np.arange(2 * LANES)[:, None] % LANES == head_of_lane[None, :])
    expand = jnp.asarray(expand, dtype=BF16)

    return pl.pallas_call(
        _post_kernel,
        grid=(n // POST_TM,),
        in_specs=[row(D_MODEL)] + [strided(d, ATTN_WIDTH) for d in GROUP_DILATIONS]
        + [strided(d, LANES) for d in GROUP_DILATIONS] + [
            row(SGU_WIDTH), row(2 * D_MODEL),
            _resident((ATTN_WIDTH, D_MODEL)), _resident((SGU_WIDTH, D_MODEL)),
            _resident((D_MODEL, D_MODEL)), _resident((1, D_MODEL)),
            _resident((D_MODEL, D_FF)), _resident((D_MODEL, D_FF)), _resident((D_FF, D_MODEL)),
            _resident((1, D_MODEL)), _resident((2 * LANES, ATTN_WIDTH)),
        ],
        out_specs=row(D_MODEL),
        out_shape=jax.ShapeDtypeStruct((n, D_MODEL), F32),
        scratch_shapes=[pltpu.VMEM((POST_TM, D_FF), BF16),
                        pltpu.VMEM((2, 1, POST_TM, LANES), F32),
                        pltpu.VMEM((2, ATTN_WIDTH // LANES, POST_TM, LANES), F32)],
        compiler_params=pltpu.CompilerParams(
            dimension_semantics=("arbitrary",), vmem_limit_bytes=VMEM_LIMIT),
        name="post",
    )(x2, *os_, *lses, sgu, gates, wpa, wps, wo, g2, wg, wu, wd, gf, expand)


def _rope_freq_row():
    inv_freq = ROPE_THETA ** (-jnp.arange(0, ROPE_DIM, 2, dtype=F32) / ROPE_DIM)
    per_head = jnp.concatenate([inv_freq, inv_freq, jnp.zeros((HEAD_DIM - ROPE_DIM,), F32)])
    return jnp.tile(per_head, LANES // HEAD_DIM)[None, :]


def kernel(x, positions, norm1_g, w_in, sgu_ln_g, sgu_ln_b, w_spatial, b_spatial, w_proj_attn,
           w_proj_sgu, w_out, norm2_g, w_ffn_gate, w_ffn_up, w_ffn_down, final_g):
    b, s, d = x.shape
    assert w_in.shape[0] == 1, "single-layer block"
    x2 = x.reshape(b * s, d)
    pos2 = positions.reshape(b * s, 1)
    ws_all = jnp.transpose(w_spatial[0], (1, 0, 2)).reshape(SGU_CHUNK, SGU_GROUPS * SGU_CHUNK)
    bs_full = jnp.repeat(jnp.transpose(b_spatial[0]), SGU_GROUP_DIM, axis=1)
    *qkvs, sgu, gates = _in_proj(
        x2, pos2, norm1_g[0][None, :], w_in[0].astype(BF16), _rope_freq_row(),
        sgu_ln_g[0][None, :], sgu_ln_b[0][None, :], ws_all.astype(BF16), bs_full, b)
    os_, lses = [], []
    for qkv, dilation in zip(qkvs, GROUP_DILATIONS):
        length = s // dilation
        o, lse = _attn_group(qkv.reshape(b * dilation, length, 3 * ATTN_WIDTH))
        os_.append(o.reshape(b, dilation, length, ATTN_WIDTH))
        lses.append(lse.reshape(b, dilation, length, LANES))
    out = _post(x2, os_, lses, sgu, gates,
                w_proj_attn[0].astype(BF16), w_proj_sgu[0].astype(BF16), w_out[0].astype(BF16),
                norm2_g[0][None, :], w_ffn_gate[0].astype(BF16), w_ffn_up[0].astype(BF16),
                w_ffn_down[0].astype(BF16), final_g[None, :], b)
    return out.reshape(b, s, d)
```
